```python
import math
import jax, jax.numpy as jnp
from jax import lax
import numpy as np

D_MODEL = 2048
BATCH = 2
SEQ = 16384
DEPTH = 1
DEC_BATCH = 16
DEC_SEQ = 64
PAST_LEN = 1024

CHUNK = 64
Q_BLOCK = 128
RET_HEADS = 8
RET_DK = 128
RET_DV = 128
RET_QK_W = RET_HEADS * RET_DK
RET_W = RET_HEADS * RET_DV
DA_HEADS = 8
DA_HEAD_DIM = 64
DA_QK_W = 2 * DA_HEADS * DA_HEAD_DIM
DA_V_W = DA_HEADS * 2 * DA_HEAD_DIM
D_FF = 4 * D_MODEL
N_IN = 2 * RET_QK_W + 2 * RET_W + 2 * DA_QK_W + DA_V_W + 2 * D_MODEL
ROPE_BASE = 10000.0
NORM_EPS = 1e-6

kernel_name = "hybrid_retention_diffattn_stream_step"


def _rms(x, g):
    xf = x.astype(jnp.float32)
    y = xf * lax.rsqrt(jnp.mean(xf * xf, axis=-1, keepdims=True) + NORM_EPS)
    return (y * g.astype(jnp.float32)).astype(x.dtype)


def _rope(x, pos):
    half = x.shape[-1] // 2
    inv = jnp.exp(-math.log(ROPE_BASE) * jnp.arange(half, dtype=jnp.float32) / half)
    ang = pos[:, None] * inv[None, :]
    cos = jnp.cos(ang)[None, :, None, :]
    sin = jnp.sin(ang)[None, :, None, :]
    x1, x2 = x[..., :half], x[..., half:]
    return jnp.concatenate([x1 * cos - x2 * sin, x2 * cos + x1 * sin], axis=-1)


def _ret_log_decay():
    return jnp.log1p(-jnp.exp2(-5.0 - jnp.arange(RET_HEADS, dtype=jnp.float32)))


def _alibi_slopes():
    return jnp.exp2(-8.0 * jnp.arange(1, DA_HEADS + 1, dtype=jnp.float32) / DA_HEADS)


def _ret_chunk(state, q, k, v, log_g):
    L = q.shape[1]
    i = jnp.arange(L, dtype=jnp.float32)
    diff = i[:, None] - i[None, :]
    causal = diff >= 0
    dmask = jnp.where(causal[None], jnp.exp(jnp.where(causal, diff, 0.0)[None] * log_g[:, None, None]), 0.0)
    scores = jnp.einsum('blhk,bmhk->bhlm', q, k) * dmask[None]
    intra = jnp.einsum('bhlm,bmhv->blhv', scores, v)
    inter = jnp.einsum('blhk,bhkv->blhv', q, state) * jnp.exp((i[:, None] + 1.0) * log_g[None, :])[None, :, :, None]
    k_w = k * jnp.exp((L - 1.0 - i)[:, None] * log_g[None, :])[None, :, :, None]
    new_state = jnp.exp(L * log_g)[None, :, None, None] * state + jnp.einsum('blhk,blhv->bhkv', k_w, v)
    return new_state, intra + inter


def _retention_prompt(q, k, v, log_g):
    B, S = q.shape[0], q.shape[1]
    n = S // CHUNK
    def to_chunks(a):
        return jnp.moveaxis(a.reshape((B, n, CHUNK) + a.shape[2:]), 1, 0)
    def step(s, c):
        return _ret_chunk(s, c[0], c[1], c[2], log_g)
    s0 = jnp.zeros((B, RET_HEADS, RET_DK, RET_DV), jnp.float32)
    s_fin, outs = lax.scan(step, s0, (to_chunks(q), to_chunks(k), to_chunks(v)))
    return jnp.moveaxis(outs, 0, 1).reshape(B, S, RET_HEADS, RET_DV), s_fin


def _diff_core(q, k, v, q_pos, k_pos, lam, slopes):
    logits = jnp.einsum('bqchd,bkchd->bchqk', q, k).astype(jnp.float32) * (DA_HEAD_DIM ** -0.5)
    dist = jnp.abs(q_pos[:, None] - k_pos[None, :]).astype(jnp.float32)
    bias = -slopes[:, None, None] * dist[None]
    allowed = (k_pos[None, :] // CHUNK) <= (q_pos[:, None] // CHUNK)
    logits = jnp.where(allowed[None, None, None], logits + bias[None, None], -jnp.inf)
    p = jax.nn.softmax(logits, axis=-1)
    a = p[:, 0] - lam * p[:, 1]
    return jnp.einsum('bhqk,bkhe->bqhe', a.astype(v.dtype), v)


def _diff_prompt(q, k, v, lam, slopes):
    B, S = q.shape[0], q.shape[1]
    nb = S // Q_BLOCK
    qb = jnp.moveaxis(q.reshape((B, nb, Q_BLOCK) + q.shape[2:]), 1, 0)
    pos = jnp.arange(S, dtype=jnp.int32).reshape(nb, Q_BLOCK)
    k_pos = jnp.arange(S, dtype=jnp.int32)
    out = lax.map(lambda a: _diff_core(a[0], k, v, a[1], k_pos, lam, slopes), (qb, pos))
    return jnp.moveaxis(out, 0, 1).reshape(B, S, DA_HEADS, 2 * DA_HEAD_DIM)


def _branch_inputs(x, pos, g_mix, w_in, qn_g, kn_g):
    B, L = x.shape[0], x.shape[1]
    h = _rms(x, g_mix)
    p = h @ w_in
    sizes = [RET_QK_W, RET_QK_W, RET_W, RET_W, DA_QK_W, DA_QK_W, DA_V_W, D_MODEL, D_MODEL]
    parts, off = [], 0
    for s in sizes:
        parts.append(p[..., off:off + s])
        off += s
    qr, kr, vr, zr, qd, kd, vd, gr, gd = parts
    qr = _rope(qr.reshape(B, L, RET_HEADS, RET_DK).astype(jnp.float32), pos)
    kr = _rope(kr.reshape(B, L, RET_HEADS, RET_DK).astype(jnp.float32), pos) * (RET_DK ** -0.5)
    vr = vr.reshape(B, L, RET_HEADS, RET_DV).astype(jnp.float32)
    qd = _rms(qd.reshape(B, L, 2, DA_HEADS, DA_HEAD_DIM), qn_g)
    kd = _rms(kd.reshape(B, L, 2, DA_HEADS, DA_HEAD_DIM), kn_g)
    vd = vd.reshape(B, L, DA_HEADS, 2 * DA_HEAD_DIM)
    return qr, kr, vr, zr, qd, kd, vd, gr, gd


def _merge_and_mlp(x, o_r, zr, o_d, gr, gd, ret_gn, da_gn, lam_init,
                   w_ret_o, w_da_o, w_out, g_mlp, w_up, w_down):
    B, L = x.shape[0], x.shape[1]
    yr = _rms(o_r, ret_gn.reshape(RET_HEADS, RET_DV)).reshape(B, L, RET_W)
    yr = (yr * jax.nn.silu(zr.astype(jnp.float32))).astype(x.dtype) @ w_ret_o
    yd = (_rms(o_d, da_gn.reshape(DA_HEADS, 2 * DA_HEAD_DIM)) * (1.0 - lam_init)).reshape(B, L, DA_V_W).astype(x.dtype) @ w_da_o
    m = jax.nn.sigmoid(gr) * yr + jax.nn.sigmoid(gd) * yd
    x = x + m @ w_out
    h2 = _rms(x, g_mlp)
    return x + jnp.square(jax.nn.relu(h2 @ w_up)) @ w_down


def setup_inputs(seed: int = 0) -> dict:
    key = jax.random.key(seed)
    ks = jax.random.split(key, 24)
    f32 = jnp.float32
    def nrm(k, shape, scale):
        return jax.random.normal(k, shape, f32) * scale
    def gain(k, shape):
        return 1.0 + 0.02 * jax.random.normal(k, shape, f32)
    return {
        "x_prompt": nrm(ks[0], (BATCH, SEQ, D_MODEL), 1.0),
        "x_sample": nrm(ks[1], (DEC_BATCH, DEC_SEQ, D_MODEL), 1.0),
        "cache_da_k": nrm(ks[2], (DEPTH, DEC_BATCH, PAST_LEN, 2 * DA_HEADS, DA_HEAD_DIM), 1.0),
        "cache_da_v": nrm(ks[3], (DEPTH, DEC_BATCH, PAST_LEN, DA_HEADS, 2 * DA_HEAD_DIM), 1.0),
        "state_ret": nrm(ks[4], (DEPTH, DEC_BATCH, RET_HEADS, RET_DK, RET_DV), 1.0),
        "g_mix": gain(ks[5], (DEPTH, D_MODEL)),
        "w_in": nrm(ks[6], (DEPTH, D_MODEL, N_IN), D_MODEL ** -0.5),
        "ret_gn": gain(ks[7], (DEPTH, RET_W)),
        "qn_g": gain(ks[8], (DEPTH, DA_HEAD_DIM)),
        "kn_g": gain(ks[9], (DEPTH, DA_HEAD_DIM)),
        "lambda_q1": nrm(ks[10], (DEPTH, DA_HEAD_DIM), 0.1),
        "lambda_k1": nrm(ks[11], (DEPTH, DA_HEAD_DIM), 0.1),
        "lambda_q2": nrm(ks[12], (DEPTH, DA_HEAD_DIM), 0.1),
        "lambda_k2": nrm(ks[13], (DEPTH, DA_HEAD_DIM), 0.1),
        "da_gn": gain(ks[14], (DEPTH, DA_V_W)),
        "w_ret_o": nrm(ks[15], (DEPTH, RET_W, D_MODEL), RET_W ** -0.5),
        "w_da_o": nrm(ks[16], (DEPTH, DA_V_W, D_MODEL), DA_V_W ** -0.5),
        "w_out": nrm(ks[17], (DEPTH, D_MODEL, D_MODEL), D_MODEL ** -0.5),
        "g_mlp": gain(ks[18], (DEPTH, D_MODEL)),
        "w_up": nrm(ks[19], (DEPTH, D_MODEL, D_FF), D_MODEL ** -0.5),
        "w_down": nrm(ks[20], (DEPTH, D_FF, D_MODEL), D_FF ** -0.5),
    }


def reference(x_prompt, x_sample, cache_da_k, cache_da_v, state_ret, g_mix, w_in, ret_gn, qn_g, kn_g,
              lambda_q1, lambda_k1, lambda_q2, lambda_k2, da_gn, w_ret_o, w_da_o, w_out, g_mlp, w_up, w_down):
    log_g = _ret_log_decay()
    slopes = _alibi_slopes()
    B, S = x_prompt.shape[0], x_prompt.shape[1]
    DB, L = x_sample.shape[0], x_sample.shape[1]
    P = cache_da_k.shape[2]
    pos_p = jnp.arange(S, dtype=jnp.float32)
    pos_s = P + jnp.arange(L, dtype=jnp.float32)
    q_pos_s = P + jnp.arange(L, dtype=jnp.int32)
    k_pos_s = jnp.arange(P + L, dtype=jnp.int32)
    xp, xs = x_prompt, x_sample
    kp_l, vp_l, rp_l, ks_l, vs_l, rs_l = [], [], [], [], [], []
    for l in range(DEPTH):
        lam_init = 0.8 - 0.6 * math.exp(-0.3 * l)
        lam = (jnp.exp(jnp.sum(lambda_q1[l].astype(jnp.float32) * lambda_k1[l].astype(jnp.float32)))
               - jnp.exp(jnp.sum(lambda_q2[l].astype(jnp.float32) * lambda_k2[l].astype(jnp.float32))) + lam_init)
        tail = (ret_gn[l], da_gn[l], lam_init, w_ret_o[l], w_da_o[l], w_out[l], g_mlp[l], w_up[l], w_down[l])
        qr, kr, vr, zr, qd, kd, vd, gr, gd = _branch_inputs(xp, pos_p, g_mix[l], w_in[l], qn_g[l], kn_g[l])
        o_r, s_fin = _retention_prompt(qr, kr, vr, log_g)
        o_d = _diff_prompt(qd, kd, vd, lam, slopes)
        xp = _merge_and_mlp(xp, o_r, zr, o_d, gr, gd, *tail)
        kp_l.append(kd.reshape(B, S, 2 * DA_HEADS, DA_HEAD_DIM))
        vp_l.append(vd)
        rp_l.append(s_fin)
        qr, kr, vr, zr, qd, kd, vd, gr, gd = _branch_inputs(xs, pos_s, g_mix[l], w_in[l], qn_g[l], kn_g[l])
        s_new, o_r = _ret_chunk(state_ret[l].astype(jnp.float32), qr, kr, vr, log_g)
        k_all = jnp.concatenate([cache_da_k[l].reshape(DB, P, 2, DA_HEADS, DA_HEAD_DIM).astype(kd.dtype), kd], axis=1)
        v_all = jnp.concatenate([cache_da_v[l].astype(vd.dtype), vd], axis=1)
        o_d = _diff_core(qd, k_all, v_all, q_pos_s, k_pos_s, lam, slopes)
        xs = _merge_and_mlp(xs, o_r, zr, o_d, gr, gd, *tail)
        ks_l.append(kd.reshape(DB, L, 2 * DA_HEADS, DA_HEAD_DIM))
        vs_l.append(vd)
        rs_l.append(s_new)
    new_k_prompt = jnp.stack(kp_l, axis=0)
    new_v_prompt = jnp.stack(vp_l, axis=0)
    ret_prompt = jnp.stack(rp_l, axis=0)
    new_k_sample = jnp.stack(ks_l, axis=0)
    new_v_sample = jnp.stack(vs_l, axis=0)
    ret_sample = jnp.stack(rs_l, axis=0)
    return (xp, xs, new_k_prompt, new_v_prompt, ret_prompt, new_k_sample, new_v_sample, ret_sample)
```

```python
import functools
import math

import numpy as np
import jax
import jax.numpy as jnp
from jax import lax
from jax.experimental import pallas as pl
from jax.experimental.pallas import tpu as pltpu

D_MODEL = 2048
CHUNK = 64
RET_HEADS = 8
RET_DK = 128
RET_DV = 128
DA_HEADS = 8
DA_HEAD_DIM = 64
D_FF = 4 * D_MODEL
ROPE_BASE = 10000.0
NORM_EPS = 1e-6
LOG2E = 1.4426950408889634

V7X_LANES = 128
V7X_MXU_DIM = 256
V7X_VMEM_LIMIT_BYTES = 56 * 1024 * 1024

COL_W = 1024
N_COL_TILES = 11
COL_QR, COL_KR, COL_VR, COL_ZR, COL_QD, COL_KD, COL_VD = 0, 1, 2, 3, 4, 5, 6
COL_GR, COL_GD = 7, 9

TOKEN_TILE = 512
RET_CHUNK_PROMPT = 256

F32 = jnp.float32
BF16 = jnp.bfloat16

_LOG_G = [math.log1p(-(2.0 ** (-5.0 - h))) for h in range(RET_HEADS)]
_SLOPES_LOG2 = np.asarray(
    [2.0 ** (-8.0 * (h + 1) / DA_HEADS) * LOG2E for h in range(DA_HEADS)], np.float32)
_Q_SCALE_LOG2 = (DA_HEAD_DIM ** -0.5) * LOG2E
_NEG = -1e30


def _cparams(n_grid):
    return pltpu.CompilerParams(
        dimension_semantics=("arbitrary",) * n_grid,
        vmem_limit_bytes=V7X_VMEM_LIMIT_BYTES)


def _pick_tile(n, cap):
    t = cap
    while n % t:
        t //= 2
    return t


def _inproj_kernel(x_ref, g_ref, w_ref, cos_ref, sin_ref, qg_ref, kg_ref, bd_ref,
                   p_ref, nk_ref, nv_ref, *rest, emit_t):
    if emit_t:
        qt_ref, vt_ref, h_scr = rest
    else:
        (h_scr,) = rest
    j = pl.program_id(1)

    @pl.when(j == 0)
    def _():
        x = x_ref[...]
        ms = jnp.mean(x * x, axis=-1, keepdims=True)
        h_scr[...] = (x * lax.rsqrt(ms + NORM_EPS) * g_ref[...]).astype(BF16)

    acc = jnp.dot(h_scr[...], w_ref[...], preferred_element_type=F32)

    def rope(scale):
        cos = cos_ref[...]
        sin = sin_ref[...]
        for hh in range(RET_HEADS):
            sl = slice(hh * RET_DK, (hh + 1) * RET_DK)
            a = acc[:, sl]
            o = a * cos + pltpu.roll(a, RET_DK // 2, axis=1) * sin
            if scale != 1.0:
                o = o * scale
            p_ref[:, sl] = o.astype(BF16)

    def head_norm(gain_row):
        sq = (acc * acc).astype(BF16)
        parts = [
            jnp.dot(sq[:, c * V7X_MXU_DIM:(c + 1) * V7X_MXU_DIM], bd_ref[...],
                    preferred_element_type=F32)
            for c in range(COL_W // V7X_MXU_DIM)
        ]
        ms = jnp.concatenate(parts, axis=1)
        return acc * lax.rsqrt(ms + NORM_EPS) * gain_row

    @pl.when(j == COL_QR)
    def _():
        rope(1.0)

    @pl.when(j == COL_KR)
    def _():
        rope(RET_DK ** -0.5)

    @pl.when((j == COL_VR) | (j == COL_ZR) | (j >= COL_GR))
    def _():
        p_ref[...] = acc.astype(BF16)

    @pl.when(j == COL_QD)
    def _():
        n = head_norm(qg_ref[...])
        p_ref[...] = n.astype(BF16)
        if emit_t:
            qt_ref[0] = (n * _Q_SCALE_LOG2).T.astype(BF16)

    @pl.when(j == COL_KD)
    def _():
        n = head_norm(kg_ref[...])
        nk_ref[...] = n
        p_ref[...] = n.astype(BF16)

    @pl.when(j == COL_VD)
    def _():
        nv_ref[...] = acc
        p_ref[...] = acc.astype(BF16)
        if emit_t:
            vt_ref[0] = acc.T.astype(BF16)


def _inproj(x2d, g_mix, w_bf, cos_tab, sin_tab, qg16, kg16, bd, tm, emit_t):
    T = x2d.shape[0]
    n_rope = cos_tab.shape[0] // tm
    grid = (T // tm, N_COL_TILES)
    in_specs = [
        pl.BlockSpec((tm, D_MODEL), lambda i, j: (i, 0)),
        pl.BlockSpec((1, D_MODEL), lambda i, j: (0, 0)),
        pl.BlockSpec((D_MODEL, COL_W), lambda i, j: (0, j)),
        pl.BlockSpec((tm, RET_DK), lambda i, j: (i % n_rope, 0)),
        pl.BlockSpec((tm, RET_DK), lambda i, j: (i % n_rope, 0)),
        pl.BlockSpec((1, COL_W), lambda i, j: (0, 0)),
        pl.BlockSpec((1, COL_W), lambda i, j: (0, 0)),
        pl.BlockSpec((V7X_MXU_DIM, V7X_MXU_DIM), lambda i, j: (0, 0)),
    ]
    out_shape = [
        jax.ShapeDtypeStruct((T, N_COL_TILES * COL_W), BF16),
        jax.ShapeDtypeStruct((T, COL_W), F32),
        jax.ShapeDtypeStruct((T, COL_W), F32),
    ]
    out_specs = [
        pl.BlockSpec((tm, COL_W), lambda i, j: (i, j)),
        pl.BlockSpec((tm, COL_W), lambda i, j: (i, 0)),
        pl.BlockSpec((tm, COL_W), lambda i, j: (i, 0)),
    ]
    if emit_t:
        out_shape += [jax.ShapeDtypeStruct((T // tm, COL_W, tm), BF16)] * 2
        out_specs += [pl.BlockSpec((1, COL_W, tm), lambda i, j: (i, 0, 0))] * 2
    return pl.pallas_call(
        functools.partial(_inproj_kernel, emit_t=emit_t),
        grid=grid, in_specs=in_specs, out_specs=out_specs, out_shape=out_shape,
        scratch_shapes=[pltpu.VMEM((tm, D_MODEL), BF16)],
        compiler_params=_cparams(2), name="inproj_t" if emit_t else "inproj",
    )(x2d, g_mix, w_bf, cos_tab, sin_tab, qg16, kg16, bd)


def _ret_kernel(q_ref, k_ref, v_ref, z_ref, gn_ref, *rest, C, has_init):
    if has_init:
        s0_ref, y_ref, s_ref, dm_scr, rd_scr, kd_scr = rest
    else:
        y_ref, s_ref, dm_scr, rd_scr, kd_scr = rest
    b = pl.program_id(0)
    c = pl.program_id(1)

    @pl.when((b == 0) & (c == 0))
    def _():
        li = lax.broadcasted_iota(jnp.int32, (C, C), 0)
        mi = lax.broadcasted_iota(jnp.int32, (C, C), 1)
        causal = li >= mi
        diff = jnp.where(causal, (li - mi).astype(F32), 0.0)
        row = lax.broadcasted_iota(jnp.int32, (C, RET_DK), 0).astype(F32)
        for h in range(RET_HEADS):
            dm_scr[h] = jnp.where(causal, jnp.exp(diff * _LOG_G[h]), 0.0)
            rd_scr[h] = jnp.exp((row + 1.0) * _LOG_G[h])
            kd_scr[h] = jnp.exp((C - 1.0 - row) * _LOG_G[h])

    @pl.when(c == 0)
    def _():
        if has_init:
            s_ref[...] = s0_ref[...]
        else:
            s_ref[...] = jnp.zeros(s_ref.shape, F32)

    for h in range(RET_HEADS):
        sl = slice(h * RET_DK, (h + 1) * RET_DK)
        q = q_ref[:, sl]
        k = k_ref[:, sl]
        v = v_ref[:, sl]
        state = s_ref[0, h]
        sc = lax.dot_general(q, k, (((1,), (1,)), ((), ())), preferred_element_type=F32)
        sc = sc * dm_scr[h]
        intra = jnp.dot(sc.astype(BF16), v, preferred_element_type=F32)
        inter = jnp.dot(q, state.astype(BF16), preferred_element_type=F32) * rd_scr[h]
        o = intra + inter
        kw = (k.astype(F32) * kd_scr[h]).astype(BF16)
        upd = lax.dot_general(kw, v, (((0,), (0,)), ((), ())), preferred_element_type=F32)
        s_ref[0, h] = state * math.exp(C * _LOG_G[h]) + upd
        ms = jnp.mean(o * o, axis=-1, keepdims=True)
        z = z_ref[:, sl].astype(F32)
        y = o * lax.rsqrt(ms + NORM_EPS) * gn_ref[:, sl] * (z * jax.nn.sigmoid(z))
        y_ref[:, sl] = y.astype(BF16)


def _retention(p, ret_gn_row, n_seq, seq_len, C, init_state):
    n_chunks = seq_len // C
    has_init = init_state is not None

    def col(t):
        return pl.BlockSpec((C, COL_W), lambda b, c: (b * n_chunks + c, t))

    in_specs = [col(COL_QR), col(COL_KR), col(COL_VR), col(COL_ZR),
                pl.BlockSpec((1, COL_W), lambda b, c: (0, 0))]
    args = [p, p, p, p, ret_gn_row]
    state_spec = pl.BlockSpec((1, RET_HEADS, RET_DK, RET_DV), lambda b, c: (b, 0, 0, 0))
    if has_init:
        in_specs.append(state_spec)
        args.append(init_state)
    return pl.pallas_call(
        functools.partial(_ret_kernel, C=C, has_init=has_init),
        grid=(n_seq, n_chunks), in_specs=in_specs,
        out_specs=[pl.BlockSpec((C, COL_W), lambda b, c: (b * n_chunks + c, 0)), state_spec],
        out_shape=[jax.ShapeDtypeStruct((n_seq * seq_len, COL_W), BF16),
                   jax.ShapeDtypeStruct((n_seq, RET_HEADS, RET_DK, RET_DV), F32)],
        scratch_shapes=[pltpu.VMEM((RET_HEADS, C, C), F32),
                        pltpu.VMEM((RET_HEADS, C, RET_DK), F32),
                        pltpu.VMEM((RET_HEADS, C, RET_DK), F32)],
        compiler_params=_cparams(2), name="retention_init" if has_init else "retention",
    )(*args)


def _lambda_value(lq1, lk1, lq2, lk2, lam_init):
    a = jnp.sum(lq1[...] * lk1[...], axis=-1, keepdims=True)
    b = jnp.sum(lq2[...] * lk2[...], axis=-1, keepdims=True)
    return jnp.exp(a) - jnp.exp(b) + lam_init


def _attn_kernel(qi_tab, ki_tab, slopes_ref, qt1_ref, qt2_ref, k1_ref, k2_ref, vt_ref,
                 lq1, lk1, lq2, lk2, gn_ref, o_ref,
                 qm_scr, m_scr, l_scr, acc_scr, rel_scr, *, blk, lam_init):
    j = pl.program_id(1)
    t = pl.program_id(2)
    qi = qi_tab[t]
    ki = ki_tab[t]
    half = DA_HEAD_DIM
    hd2 = 2 * DA_HEAD_DIM

    @pl.when(ki == 0)
    def _():
        zeros = jnp.zeros((half, blk), BF16)
        for c, qref in enumerate((qt1_ref, qt2_ref)):
            qm_scr[2 * c, :half, :] = qref[0, :half, :]
            qm_scr[2 * c, half:, :] = zeros
            qm_scr[2 * c + 1, :half, :] = zeros
            qm_scr[2 * c + 1, half:, :] = qref[0, half:, :]
        m_scr[...] = jnp.full(m_scr.shape, _NEG, F32)
        l_scr[...] = jnp.zeros(l_scr.shape, F32)
        acc_scr[...] = jnp.zeros(acc_scr.shape, F32)
        kk = lax.broadcasted_iota(jnp.int32, (blk, blk), 0)
        qq = lax.broadcasted_iota(jnp.int32, (blk, blk), 1)
        rel_scr[...] = (kk - qq).astype(F32)

    def step(diag):
        for c in range(2):
            kc = (k1_ref, k2_ref)[c][...]
            for hh in range(2):
                idx = 2 * c + hh
                slope = slopes_ref[2 * j + hh]
                s = jnp.dot(kc, qm_scr[idx], preferred_element_type=F32)
                if diag:
                    kk = lax.broadcasted_iota(jnp.int32, (blk, blk), 0)
                    qq = lax.broadcasted_iota(jnp.int32, (blk, blk), 1)
                    allowed = (kk // CHUNK) <= (qq // CHUNK)
                    logit = jnp.where(allowed, s - slope * jnp.abs(rel_scr[...]), _NEG)
                    shift = 0.0
                else:
                    logit = s + slope * rel_scr[...]
                    shift = slope * ((qi - ki) * blk).astype(F32)
                m_old = m_scr[idx]
                m_new = jnp.maximum(m_old, jnp.max(logit, axis=0, keepdims=True) - shift)
                alpha = jnp.exp2(m_old - m_new)
                p = jnp.exp2(logit - (m_new + shift))
                l_scr[idx] = alpha * l_scr[idx] + jnp.sum(p, axis=0, keepdims=True)
                vt_h = vt_ref[0, hh * hd2:(hh + 1) * hd2, :]
                acc_scr[idx] = alpha * acc_scr[idx] + jnp.dot(
                    vt_h, p.astype(BF16), preferred_element_type=F32)
                m_scr[idx] = m_new

    @pl.when(ki < qi)
    def _():
        step(False)

    @pl.when(ki == qi)
    def _():
        step(True)
        lam = _lambda_value(lq1, lk1, lq2, lk2, lam_init)
        for hh in range(2):
            o = acc_scr[hh] / l_scr[hh] - lam * (acc_scr[2 + hh] / l_scr[2 + hh])
            ms = jnp.mean(o * o, axis=0, keepdims=True)
            on = (o * lax.rsqrt(ms + NORM_EPS)).T
            sl = slice(hh * hd2, (hh + 1) * hd2)
            o_ref[:, sl] = (on * gn_ref[:, sl] * (1.0 - lam_init)).astype(BF16)


def _attention_prompt(p, qt, vt, lam_rows, da_gn_row, n_seq, seq_len, blk, lam_init):
    nb = seq_len // blk
    pairs = [(q, k) for q in range(nb) for k in range(q + 1)]
    qi_tab = jnp.asarray([a for a, _ in pairs], jnp.int32)
    ki_tab = jnp.asarray([b for _, b in pairs], jnp.int32)
    n_hp = DA_HEADS // 2
    hd2 = 2 * DA_HEAD_DIM
    kcol0 = COL_KD * COL_W // hd2

    smem = pl.BlockSpec(memory_space=pltpu.SMEM)
    in_specs = [
        smem,
        pl.BlockSpec((1, hd2, blk), lambda b, j, t, qt_, kt_: (b * nb + qt_[t], j, 0)),
        pl.BlockSpec((1, hd2, blk), lambda b, j, t, qt_, kt_: (b * nb + qt_[t], n_hp + j, 0)),
        pl.BlockSpec((blk, hd2), lambda b, j, t, qt_, kt_: (b * nb + kt_[t], kcol0 + j)),
        pl.BlockSpec((blk, hd2), lambda b, j, t, qt_, kt_: (b * nb + kt_[t], kcol0 + n_hp + j)),
        pl.BlockSpec((1, 2 * hd2, blk), lambda b, j, t, qt_, kt_: (b * nb + kt_[t], j, 0)),
    ] + [pl.BlockSpec((1, DA_HEAD_DIM), lambda b, j, t, qt_, kt_: (0, 0))] * 4 + [
        pl.BlockSpec((1, 2 * hd2), lambda b, j, t, qt_, kt_: (0, j)),
    ]
    grid_spec = pltpu.PrefetchScalarGridSpec(
        num_scalar_prefetch=2, grid=(n_seq, n_hp, len(pairs)), in_specs=in_specs,
        out_specs=pl.BlockSpec((blk, 2 * hd2), lambda b, j, t, qt_, kt_: (b * nb + qt_[t], j)),
        scratch_shapes=[pltpu.VMEM((4, hd2, blk), BF16),
                        pltpu.VMEM((4, 1, blk), F32),
                        pltpu.VMEM((4, 1, blk), F32),
                        pltpu.VMEM((4, hd2, blk), F32),
                        pltpu.VMEM((blk, blk), F32)])
    return pl.pallas_call(
        functools.partial(_attn_kernel, blk=blk, lam_init=lam_init),
        grid_spec=grid_spec,
        out_shape=jax.ShapeDtypeStruct((n_seq * seq_len, COL_W), BF16),
        compiler_params=_cparams(3), name="diff_attn_prompt",
    )(qi_tab, ki_tab, jnp.asarray(_SLOPES_LOG2), qt, qt, p, p, vt, *lam_rows, da_gn_row)


def _sattn_kernel(slopes_ref, q_ref, kn_ref, vn_ref, kc_ref, vc_ref,
                  lq1, lk1, lq2, lk2, gn_ref, o_ref, *, L, P, lam_init):
    lam = _lambda_value(lq1, lk1, lq2, lk2, lam_init)
    hd2 = 2 * DA_HEAD_DIM
    lane = lax.broadcasted_iota(jnp.int32, (L, hd2), 1)
    qpos_c = P + lax.broadcasted_iota(jnp.int32, (L, P), 0)
    kpos_c = lax.broadcasted_iota(jnp.int32, (L, P), 1)
    qpos_n = P + lax.broadcasted_iota(jnp.int32, (L, L), 0)
    kpos_n = P + lax.broadcasted_iota(jnp.int32, (L, L), 1)
    dist_c = jnp.abs(qpos_c - kpos_c).astype(F32)
    dist_n = jnp.abs(qpos_n - kpos_n).astype(F32)
    ok_c = (kpos_c // CHUNK) <= (qpos_c // CHUNK)
    ok_n = (kpos_n // CHUNK) <= (qpos_n // CHUNK)
    nt = (((1,), (1,)), ((), ()))
    for pr in range(DA_HEADS // 2):
        for hh in range(2):
            h = 2 * pr + hh
            slope = slopes_ref[h]
            probs = []
            for c in range(2):
                sl = slice((c * (DA_HEADS // 2) + pr) * hd2, (c * (DA_HEADS // 2) + pr + 1) * hd2)
                q128 = q_ref[:, sl].astype(F32) * _Q_SCALE_LOG2
                keep = (lane < DA_HEAD_DIM) if hh == 0 else (lane >= DA_HEAD_DIM)
                qm = jnp.where(keep, q128, 0.0).astype(BF16)
                kc = kc_ref[0, :, sl].astype(BF16)
                kn = kn_ref[:, sl]
                s_c = lax.dot_general(qm, kc, nt, preferred_element_type=F32)
                s_n = lax.dot_general(qm, kn, nt, preferred_element_type=F32)
                s_c = jnp.where(ok_c, s_c - slope * dist_c, _NEG)
                s_n = jnp.where(ok_n, s_n - slope * dist_n, _NEG)
                m = jnp.maximum(jnp.max(s_c, axis=-1, keepdims=True),
                                jnp.max(s_n, axis=-1, keepdims=True))
                p_c = jnp.exp2(s_c - m)
                p_n = jnp.exp2(s_n - m)
                inv = 1.0 / (jnp.sum(p_c, axis=-1, keepdims=True)
                             + jnp.sum(p_n, axis=-1, keepdims=True))
                probs.append((p_c * inv, p_n * inv))
            a_c = (probs[0][0] - lam * probs[1][0]).astype(BF16)
            a_n = (probs[0][1] - lam * probs[1][1]).astype(BF16)
            vsl = slice(h * hd2, (h + 1) * hd2)
            o = (jnp.dot(a_c, vc_ref[0, :, vsl].astype(BF16), preferred_element_type=F32)
                 + jnp.dot(a_n, vn_ref[:, vsl], preferred_element_type=F32))
            ms = jnp.mean(o * o, axis=-1, keepdims=True)
            on = o * lax.rsqrt(ms + NORM_EPS) * gn_ref[:, vsl] * (1.0 - lam_init)
            o_ref[:, vsl] = on.astype(BF16)


def _attention_sample(p, cache_k, cache_v, lam_rows, da_gn_row, n_seq, L, lam_init):
    P = cache_k.shape[1]
    smem = pl.BlockSpec(memory_space=pltpu.SMEM)
    in_specs = [
        smem,
        pl.BlockSpec((L, COL_W), lambda s: (s, COL_QD)),
        pl.BlockSpec((L, COL_W), lambda s: (s, COL_KD)),
        pl.BlockSpec((L, COL_W), lambda s: (s, COL_VD)),
        pl.BlockSpec((1, P, COL_W), lambda s: (s, 0, 0)),
        pl.BlockSpec((1, P, COL_W), lambda s: (s, 0, 0)),
    ] + [pl.BlockSpec((1, DA_HEAD_DIM), lambda s: (0, 0))] * 4 + [
        pl.BlockSpec((1, COL_W), lambda s: (0, 0)),
    ]
    return pl.pallas_call(
        functools.partial(_sattn_kernel, L=L, P=P, lam_init=lam_init),
        grid=(n_seq,), in_specs=in_specs,
        out_specs=pl.BlockSpec((L, COL_W), lambda s: (s, 0)),
        out_shape=jax.ShapeDtypeStruct((n_seq * L, COL_W), BF16),
        compiler_params=_cparams(1), name="diff_attn_sample",
    )(jnp.asarray(_SLOPES_LOG2), p, p, p, cache_k, cache_v, *lam_rows, da_gn_row)


def _merge_kernel(x_ref, yr_ref, yd_ref, gr0, gr1, gd0, gd1, wr_ref, wd_ref, wo_ref, gm_ref,
                  x1_ref, h2_ref):
    yr = jnp.dot(yr_ref[...], wr_ref[...], preferred_element_type=F32)
    yd = jnp.dot(yd_ref[...], wd_ref[...], preferred_element_type=F32)
    gr = jnp.concatenate([gr0[...], gr1[...]], axis=1).astype(F32)
    gd = jnp.concatenate([gd0[...], gd1[...]], axis=1).astype(F32)
    m = jax.nn.sigmoid(gr) * yr + jax.nn.sigmoid(gd) * yd
    x1 = x_ref[...] + jnp.dot(m.astype(BF16), wo_ref[...], preferred_element_type=F32)
    x1_ref[...] = x1
    ms = jnp.mean(x1 * x1, axis=-1, keepdims=True)
    h2_ref[...] = (x1 * lax.rsqrt(ms + NORM_EPS) * gm_ref[...]).astype(BF16)


def _merge(x2d, yr_in, yd_in, p, w_ret_o, w_da_o, w_out, g_mlp_row, tm):
    T = x2d.shape[0]

    def whole(shape):
        return pl.BlockSpec(shape, lambda i: (0, 0), pipeline_mode=pl.Buffered(1))

    def pcol(t):
        return pl.BlockSpec((tm, COL_W), lambda i: (i, t))

    in_specs = [
        pl.BlockSpec((tm, D_MODEL), lambda i: (i, 0)),
        pl.BlockSpec((tm, COL_W), lambda i: (i, 0)),
        pl.BlockSpec((tm, COL_W), lambda i: (i, 0)),
        pcol(COL_GR), pcol(COL_GR + 1), pcol(COL_GD), pcol(COL_GD + 1),
        whole((COL_W, D_MODEL)), whole((COL_W, D_MODEL)), whole((D_MODEL, D_MODEL)),
        pl.BlockSpec((1, D_MODEL), lambda i: (0, 0)),
    ]
    return pl.pallas_call(
        _merge_kernel, grid=(T // tm,), in_specs=in_specs,
        out_specs=[pl.BlockSpec((tm, D_MODEL), lambda i: (i, 0)),
                   pl.BlockSpec((tm, D_MODEL), lambda i: (i, 0))],
        out_shape=[jax.ShapeDtypeStruct((T, D_MODEL), F32),
                   jax.ShapeDtypeStruct((T, D_MODEL), BF16)],
        compiler_params=_cparams(1), name="merge",
    )(x2d, yr_in, yd_in, p, p, p, p, w_ret_o, w_da_o, w_out, g_mlp_row)


def _mlp_kernel(x1_ref, h2_ref, wu_ref, wd_ref, y_ref):
    f = pl.program_id(1)
    u = jnp.dot(h2_ref[...], wu_ref[...], preferred_element_type=F32)
    r = jnp.square(jnp.maximum(u, 0.0)).astype(BF16)
    contrib = jnp.dot(r, wd_ref[...], preferred_element_type=F32)

    @pl.when(f == 0)
    def _():
        y_ref[...] = x1_ref[...] + contrib

    @pl.when(f > 0)
    def _():
        y_ref[...] += contrib


def _mlp(x1, h2, w_up, w_down, tm, tf):
    T = x1.shape[0]
    return pl.pallas_call(
        _mlp_kernel, grid=(T // tm, D_FF // tf),
        in_specs=[pl.BlockSpec((tm, D_MODEL), lambda i, f: (i, 0)),
                  pl.BlockSpec((tm, D_MODEL), lambda i, f: (i, 0)),
                  pl.BlockSpec((D_MODEL, tf), lambda i, f: (0, f)),
                  pl.BlockSpec((tf, D_MODEL), lambda i, f: (f, 0))],
        out_specs=pl.BlockSpec((tm, D_MODEL), lambda i, f: (i, 0)),
        out_shape=jax.ShapeDtypeStruct((T, D_MODEL), F32),
        compiler_params=_cparams(2), name="mlp",
    )(x1, h2, w_up, w_down)


def _rope_tables(pos):
    half = RET_DK // 2
    inv = jnp.exp(-math.log(ROPE_BASE) * jnp.arange(half, dtype=F32) / half)
    ang = pos[:, None] * inv[None, :]
    cos = jnp.cos(ang)
    sin = jnp.sin(ang)
    return jnp.concatenate([cos, cos], axis=1), jnp.concatenate([-sin, sin], axis=1)


def _block_diag_mean():
    r = np.arange(V7X_MXU_DIM) // DA_HEAD_DIM
    return jnp.asarray((r[:, None] == r[None, :]).astype(np.float32) / DA_HEAD_DIM, BF16)


def kernel(x_prompt, x_sample, cache_da_k, cache_da_v, state_ret, g_mix, w_in, ret_gn, qn_g, kn_g,
           lambda_q1, lambda_k1, lambda_q2, lambda_k2, da_gn, w_ret_o, w_da_o, w_out, g_mlp,
           w_up, w_down):
    B, S, _ = x_prompt.shape
    DB, L, _ = x_sample.shape
    P = cache_da_k.shape[2]
    depth = w_in.shape[0]
    assert S % TOKEN_TILE == 0 and S % RET_CHUNK_PROMPT == 0 and L % 16 == 0

    tm_p = TOKEN_TILE
    tm_s = _pick_tile(DB * L, TOKEN_TILE)
    assert tm_s % L == 0 or L % tm_s == 0
    cos_p, sin_p = _rope_tables(jnp.arange(S, dtype=F32))
    cos_s, sin_s = _rope_tables(P + jnp.arange(L, dtype=F32))
    if tm_s > L:
        cos_s = jnp.tile(cos_s, (tm_s // L, 1))
        sin_s = jnp.tile(sin_s, (tm_s // L, 1))
    bd = _block_diag_mean()
    n_heads_qk = COL_W // DA_HEAD_DIM

    xp = x_prompt.reshape(B * S, D_MODEL)
    xs = x_sample.reshape(DB * L, D_MODEL)
    kp_l, vp_l, rp_l, ks_l, vs_l, rs_l = [], [], [], [], [], []
    for l in range(depth):
        lam_init = 0.8 - 0.6 * math.exp(-0.3 * l)
        w_in_bf = w_in[l].astype(BF16)
        w_ro, w_do, w_o = w_ret_o[l].astype(BF16), w_da_o[l].astype(BF16), w_out[l].astype(BF16)
        w_u, w_d = w_up[l].astype(BF16), w_down[l].astype(BF16)
        g_mix_row = g_mix[l].reshape(1, D_MODEL)
        g_mlp_row = g_mlp[l].reshape(1, D_MODEL)
        qg16 = jnp.tile(qn_g[l], n_heads_qk).reshape(1, COL_W)
        kg16 = jnp.tile(kn_g[l], n_heads_qk).reshape(1, COL_W)
        ret_gn_row = ret_gn[l].reshape(1, COL_W)
        da_gn_row = da_gn[l].reshape(1, COL_W)
        lam_rows = [a[l].astype(F32).reshape(1, DA_HEAD_DIM)
                    for a in (lambda_q1, lambda_k1, lambda_q2, lambda_k2)]

        p, nk, nv, qt, vt = _inproj(xp, g_mix_row, w_in_bf, cos_p, sin_p, qg16, kg16, bd,
                                    tm_p, True)
        yr_in, s_fin = _retention(p, ret_gn_row, B, S, RET_CHUNK_PROMPT, None)
        yd_in = _attention_prompt(p, qt, vt, lam_rows, da_gn_row, B, S, TOKEN_TILE, lam_init)
        x1, h2 = _merge(xp, yr_in, yd_in, p, w_ro, w_do, w_o, g_mlp_row, tm_p)
        xp = _mlp(x1, h2, w_u, w_d, tm_p, COL_W)
        kp_l.append(nk.reshape(B, S, 2 * DA_HEADS, DA_HEAD_DIM))
        vp_l.append(nv.reshape(B, S, DA_HEADS, 2 * DA_HEAD_DIM))
        rp_l.append(s_fin)

        p, nk, nv = _inproj(xs, g_mix_row, w_in_bf, cos_s, sin_s, qg16, kg16, bd, tm_s, False)
        yr_in, s_new = _retention(p, ret_gn_row, DB, L, L, state_ret[l].astype(F32))
        yd_in = _attention_sample(p, cache_da_k[l].reshape(DB, P, COL_W),
                                  cache_da_v[l].reshape(DB, P, COL_W),
                                  lam_rows, da_gn_row, DB, L, lam_init)
        x1, h2 = _merge(xs, yr_in, yd_in, p, w_ro, w_do, w_o, g_mlp_row, tm_s)
        xs = _mlp(x1, h2, w_u, w_d, tm_s, COL_W)
        ks_l.append(nk.reshape(DB, L, 2 * DA_HEADS, DA_HEAD_DIM))
        vs_l.append(nv.reshape(DB, L, DA_HEADS, 2 * DA_HEAD_DIM))
        rs_l.append(s_new)

    return (xp.reshape(B, S, D_MODEL), xs.reshape(DB, L, D_MODEL),
            jnp.stack(kp_l, 0), jnp.stack(vp_l, 0), jnp.stack(rp_l, 0),
            jnp.stack(ks_l, 0), jnp.stack(vs_l, 0), jnp.stack(rs_l, 0))
```

```python
import functools
import math

import numpy as np
import jax
import jax.numpy as jnp
from jax import lax
from jax.experimental import pallas as pl
from jax.experimental.pallas import tpu as pltpu

D_MODEL = 2048
CHUNK = 64
RET_HEADS = 8
RET_DK = 128
RET_DV = 128
DA_HEADS = 8
DA_HEAD_DIM = 64
D_FF = 4 * D_MODEL
ROPE_BASE = 10000.0
NORM_EPS = 1e-6
LOG2E = 1.4426950408889634

V7X_LANES = 128
V7X_MXU_DIM = 256
V7X_VMEM_LIMIT_BYTES = 56 * 1024 * 1024

COL_W = 1024
N_COL_TILES = 11
COL_QR, COL_KR, COL_VR, COL_ZR, COL_QD, COL_KD, COL_VD = 0, 1, 2, 3, 4, 5, 6
COL_GR, COL_GD = 7, 9

TOKEN_TILE = 512
RET_CHUNK_PROMPT = 256

F32 = jnp.float32
BF16 = jnp.bfloat16

_LOG_G = [math.log1p(-(2.0 ** (-5.0 - h))) for h in range(RET_HEADS)]
_SLOPES = np.asarray([2.0 ** (-8.0 * (h + 1) / DA_HEADS) for h in range(DA_HEADS)], np.float32)
_SLOPES_LOG2 = (_SLOPES * np.float32(LOG2E)).astype(np.float32)
_Q_SCALE_LOG2 = (DA_HEAD_DIM ** -0.5) * LOG2E
_NEG = -1e30


def _cparams(n_grid):
    return pltpu.CompilerParams(
        dimension_semantics=("arbitrary",) * n_grid,
        vmem_limit_bytes=V7X_VMEM_LIMIT_BYTES)


def _pick_tile(n, cap):
    t = cap
    while n % t:
        t //= 2
    return t


def _inproj_kernel(x_ref, g_ref, w_ref, cos_ref, sin_ref, qg_ref, kg_ref, bd_ref,
                   p_ref, nk_ref, nv_ref, *rest, emit_t):
    if emit_t:
        qt_ref, vt_ref, h_scr = rest
    else:
        (h_scr,) = rest
    j = pl.program_id(1)

    @pl.when(j == 0)
    def _():
        x = x_ref[...]
        ms = jnp.mean(x * x, axis=-1, keepdims=True)
        h_scr[...] = (x * lax.rsqrt(ms + NORM_EPS) * g_ref[...]).astype(BF16)

    acc = jnp.dot(h_scr[...], w_ref[...], preferred_element_type=F32)

    def rope(scale):
        cos = cos_ref[...]
        sin = sin_ref[...]
        for hh in range(RET_HEADS):
            sl = slice(hh * RET_DK, (hh + 1) * RET_DK)
            a = acc[:, sl]
            o = a * cos + pltpu.roll(a, RET_DK // 2, axis=1) * sin
            if scale != 1.0:
                o = o * scale
            p_ref[:, sl] = o.astype(BF16)

    def head_norm(gain_row):
        sq = (acc * acc).astype(BF16)
        parts = [
            jnp.dot(sq[:, c * V7X_MXU_DIM:(c + 1) * V7X_MXU_DIM], bd_ref[...],
                    preferred_element_type=F32)
            for c in range(COL_W // V7X_MXU_DIM)
        ]
        ms = jnp.concatenate(parts, axis=1)
        return acc * lax.rsqrt(ms + NORM_EPS) * gain_row

    @pl.when(j == COL_QR)
    def _():
        rope(1.0)

    @pl.when(j == COL_KR)
    def _():
        rope(RET_DK ** -0.5)

    @pl.when((j == COL_VR) | (j == COL_ZR) | (j >= COL_GR))
    def _():
        p_ref[...] = acc.astype(BF16)

    @pl.when(j == COL_QD)
    def _():
        n = head_norm(qg_ref[...])
        p_ref[...] = n.astype(BF16)
        if emit_t:
            qt_ref[0] = (n * _Q_SCALE_LOG2).T.astype(BF16)

    @pl.when(j == COL_KD)
    def _():
        n = head_norm(kg_ref[...])
        nk_ref[...] = n
        p_ref[...] = n.astype(BF16)

    @pl.when(j == COL_VD)
    def _():
        nv_ref[...] = acc
        p_ref[...] = acc.astype(BF16)
        if emit_t:
            vt_ref[0] = acc.T.astype(BF16)


def _inproj(x2d, g_mix, w_bf, cos_tab, sin_tab, qg16, kg16, bd, tm, emit_t):
    T = x2d.shape[0]
    n_rope = cos_tab.shape[0] // tm
    grid = (T // tm, N_COL_TILES)
    in_specs = [
        pl.BlockSpec((tm, D_MODEL), lambda i, j: (i, 0)),
        pl.BlockSpec((1, D_MODEL), lambda i, j: (0, 0)),
        pl.BlockSpec((D_MODEL, COL_W), lambda i, j: (0, j)),
        pl.BlockSpec((tm, RET_DK), lambda i, j: (i % n_rope, 0)),
        pl.BlockSpec((tm, RET_DK), lambda i, j: (i % n_rope, 0)),
        pl.BlockSpec((1, COL_W), lambda i, j: (0, 0)),
        pl.BlockSpec((1, COL_W), lambda i, j: (0, 0)),
        pl.BlockSpec((V7X_MXU_DIM, V7X_MXU_DIM), lambda i, j: (0, 0)),
    ]
    out_shape = [
        jax.ShapeDtypeStruct((T, N_COL_TILES * COL_W), BF16),
        jax.ShapeDtypeStruct((T, COL_W), F32),
        jax.ShapeDtypeStruct((T, COL_W), F32),
    ]
    out_specs = [
        pl.BlockSpec((tm, COL_W), lambda i, j: (i, j)),
        pl.BlockSpec((tm, COL_W), lambda i, j: (i, 0)),
        pl.BlockSpec((tm, COL_W), lambda i, j: (i, 0)),
    ]
    if emit_t:
        out_shape += [jax.ShapeDtypeStruct((T // tm, COL_W, tm), BF16)] * 2
        out_specs += [pl.BlockSpec((1, COL_W, tm), lambda i, j: (i, 0, 0))] * 2
    return pl.pallas_call(
        functools.partial(_inproj_kernel, emit_t=emit_t),
        grid=grid, in_specs=in_specs, out_specs=out_specs, out_shape=out_shape,
        scratch_shapes=[pltpu.VMEM((tm, D_MODEL), BF16)],
        compiler_params=_cparams(2), name="inproj_t" if emit_t else "inproj",
    )(x2d, g_mix, w_bf, cos_tab, sin_tab, qg16, kg16, bd)


def _ret_kernel(q_ref, k_ref, v_ref, z_ref, gn_ref, *rest, C, has_init):
    if has_init:
        s0_ref, y_ref, s_ref, dm_scr, rd_scr, kd_scr = rest
    else:
        y_ref, s_ref, dm_scr, rd_scr, kd_scr = rest
    b = pl.program_id(0)
    c = pl.program_id(1)

    @pl.when((b == 0) & (c == 0))
    def _():
        li = lax.broadcasted_iota(jnp.int32, (C, C), 0)
        mi = lax.broadcasted_iota(jnp.int32, (C, C), 1)
        causal = li >= mi
        diff = jnp.where(causal, (li - mi).astype(F32), 0.0)
        row = lax.broadcasted_iota(jnp.int32, (C, RET_DK), 0).astype(F32)
        for h in range(RET_HEADS):
            dm_scr[h] = jnp.where(causal, jnp.exp(diff * _LOG_G[h]), 0.0)
            rd_scr[h] = jnp.exp((row + 1.0) * _LOG_G[h])
            kd_scr[h] = jnp.exp((C - 1.0 - row) * _LOG_G[h])

    @pl.when(c == 0)
    def _():
        if has_init:
            s_ref[...] = s0_ref[...]
        else:
            s_ref[...] = jnp.zeros(s_ref.shape, F32)

    for h in range(RET_HEADS):
        sl = slice(h * RET_DK, (h + 1) * RET_DK)
        q = q_ref[:, sl]
        k = k_ref[:, sl]
        v = v_ref[:, sl]
        state = s_ref[0, h]
        sc = lax.dot_general(q, k, (((1,), (1,)), ((), ())), preferred_element_type=F32)
        sc = sc * dm_scr[h]
        intra = jnp.dot(sc.astype(BF16), v, preferred_element_type=F32)
        inter = jnp.dot(q, state.astype(BF16), preferred_element_type=F32) * rd_scr[h]
        o = intra + inter
        kw = (k.astype(F32) * kd_scr[h]).astype(BF16)
        upd = lax.dot_general(kw, v, (((0,), (0,)), ((), ())), preferred_element_type=F32)
        s_ref[0, h] = state * math.exp(C * _LOG_G[h]) + upd
        ms = jnp.mean(o * o, axis=-1, keepdims=True)
        z = z_ref[:, sl].astype(F32)
        y = o * lax.rsqrt(ms + NORM_EPS) * gn_ref[:, sl] * (z * jax.nn.sigmoid(z))
        y_ref[:, sl] = y.astype(BF16)


def _retention(p, ret_gn_row, n_seq, seq_len, C, init_state):
    n_chunks = seq_len // C
    has_init = init_state is not None

    def col(t):
        return pl.BlockSpec((C, COL_W), lambda b, c: (b * n_chunks + c, t))

    in_specs = [col(COL_QR), col(COL_KR), col(COL_VR), col(COL_ZR),
                pl.BlockSpec((1, COL_W), lambda b, c: (0, 0))]
    args = [p, p, p, p, ret_gn_row]
    state_spec = pl.BlockSpec((1, RET_HEADS, RET_DK, RET_DV), lambda b, c: (b, 0, 0, 0))
    if has_init:
        in_specs.append(state_spec)
        args.append(init_state)
    return pl.pallas_call(
        functools.partial(_ret_kernel, C=C, has_init=has_init),
        grid=(n_seq, n_chunks), in_specs=in_specs,
        out_specs=[pl.BlockSpec((C, COL_W), lambda b, c: (b * n_chunks + c, 0)), state_spec],
        out_shape=[jax.ShapeDtypeStruct((n_seq * seq_len, COL_W), BF16),
                   jax.ShapeDtypeStruct((n_seq, RET_HEADS, RET_DK, RET_DV), F32)],
        scratch_shapes=[pltpu.VMEM((RET_HEADS, C, C), F32),
                        pltpu.VMEM((RET_HEADS, C, RET_DK), F32),
                        pltpu.VMEM((RET_HEADS, C, RET_DK), F32)],
        compiler_params=_cparams(2), name="retention_init" if has_init else "retention",
    )(*args)


def _lambda_value(lq1, lk1, lq2, lk2, lam_init):
    a = jnp.sum(lq1[...] * lk1[...], axis=-1, keepdims=True)
    b = jnp.sum(lq2[...] * lk2[...], axis=-1, keepdims=True)
    return jnp.exp(a) - jnp.exp(b) + lam_init


def _bf16_split3(x):
    out, r = [], np.float64(x)
    for _ in range(3):
        c = np.float64(np.asarray(r, np.float32).astype(jnp.bfloat16).astype(np.float32))
        out.append(float(c))
        r = r - c
    return out


def _alibi_operands(blk):
    c = _bf16_split3(LOG2E)
    pos = np.arange(blk)
    ek = np.zeros((blk, 2 * DA_HEAD_DIM), np.float32)
    bq = np.zeros((2 * DA_HEAD_DIM, blk), np.float32)
    for t in range(3):
        ek[:, t] = pos >> 4
        ek[:, 3 + t] = pos & 15
        ek[:, 6 + t] = -16.0 * c[t]
        ek[:, 9 + t] = -c[t]
        bq[t, :] = 16.0 * c[t]
        bq[3 + t, :] = c[t]
        bq[6 + t, :] = pos >> 4
        bq[9 + t, :] = pos & 15
    return jnp.asarray(ek, BF16), jnp.asarray(bq, BF16)


def _attn_kernel(slog2_ref, spow2_ref, qt1_ref, qt2_ref, k1_ref, k2_ref, vt_ref, ek_ref, bq_ref,
                 lq1, lk1, lq2, lk2, gn_ref, o_ref,
                 qx_scr, m_scr, l_scr, acc_scr, s_scr, mb_scr, *, blk, lam_init):
    j = pl.program_id(1)
    qi = pl.program_id(2)
    half = DA_HEAD_DIM
    hd2 = 2 * DA_HEAD_DIM
    k_refs = (k1_ref, k2_ref)

    zeros = jnp.zeros((half, blk), BF16)
    for hh in range(2):
        bias_rows = (bq_ref[...].astype(F32) * spow2_ref[2 * j + hh]).astype(BF16)
        for c, qref in enumerate((qt1_ref, qt2_ref)):
            idx = 2 * c + hh
            if hh == 0:
                qx_scr[idx, :half, :] = qref[0, :half, :]
                qx_scr[idx, half:hd2, :] = zeros
            else:
                qx_scr[idx, :half, :] = zeros
                qx_scr[idx, half:hd2, :] = qref[0, half:, :]
            qx_scr[idx, hd2:, :] = bias_rows
    m_scr[...] = jnp.full(m_scr.shape, _NEG, F32)
    l_scr[...] = jnp.zeros(l_scr.shape, F32)
    acc_scr[...] = jnp.zeros(acc_scr.shape, F32)

    def score_diag(par):
        kk = lax.broadcasted_iota(jnp.int32, (blk, blk), 0)
        qq = lax.broadcasted_iota(jnp.int32, (blk, blk), 1)
        allowed = (kk // CHUNK) <= (qq // CHUNK)
        dist = jnp.abs(kk - qq).astype(F32)
        row0 = pl.multiple_of(qi * blk, blk)
        for c in range(2):
            kc = k_refs[c][pl.ds(row0, blk), :]
            for hh in range(2):
                idx = 2 * c + hh
                s = jnp.dot(kc, qx_scr[idx, :hd2, :], preferred_element_type=F32)
                s = jnp.where(allowed, s - slog2_ref[2 * j + hh] * dist, _NEG)
                s_scr[par, idx] = s
                mb_scr[par, idx] = jnp.max(s, axis=0, keepdims=True)

    def score_off(ki, par):
        row0 = pl.multiple_of(ki * blk, blk)
        for c in range(2):
            kx = jnp.concatenate([k_refs[c][pl.ds(row0, blk), :], ek_ref[...]], axis=1)
            for hh in range(2):
                idx = 2 * c + hh
                s = jnp.dot(kx, qx_scr[idx], preferred_element_type=F32)
                s_scr[par, idx] = s
                mb_scr[par, idx] = jnp.max(s, axis=0, keepdims=True)

    def consume(kb, par):
        dblk = ((qi - kb) * blk).astype(F32)
        for c in range(2):
            for hh in range(2):
                idx = 2 * c + hh
                shift = slog2_ref[2 * j + hh] * dblk
                m_old = m_scr[idx]
                m_new = jnp.maximum(m_old, mb_scr[par, idx] - shift)
                alpha = jnp.exp2(m_old - m_new)
                p = jnp.exp2(s_scr[par, idx] - (m_new + shift))
                l_scr[idx] = alpha * l_scr[idx] + jnp.sum(p, axis=0, keepdims=True)
                vt_h = vt_ref[kb, hh * hd2:(hh + 1) * hd2, :]
                acc_scr[idx] = alpha * acc_scr[idx] + jnp.dot(
                    vt_h, p.astype(BF16), preferred_element_type=F32)
                m_scr[idx] = m_new

    score_diag(0)

    def pair_body(u, carry):
        t0 = 2 * u
        score_off(t0, 1)
        consume(jnp.where(t0 == 0, qi, t0 - 1), 0)
        score_off(t0 + 1, 0)
        consume(t0, 1)
        return carry

    lax.fori_loop(0, qi // 2, pair_body, 0)

    @pl.when(qi % 2 == 1)
    def _():
        t0 = qi - 1
        score_off(t0, 1)
        consume(jnp.where(t0 == 0, qi, t0 - 1), 0)
        consume(t0, 1)

    @pl.when(qi % 2 == 0)
    def _():
        consume(jnp.where(qi == 0, qi, qi - 1), 0)

    lam = _lambda_value(lq1, lk1, lq2, lk2, lam_init)
    for hh in range(2):
        o = acc_scr[hh] / l_scr[hh] - lam * (acc_scr[2 + hh] / l_scr[2 + hh])
        ms = jnp.mean(o * o, axis=0, keepdims=True)
        on = (o * lax.rsqrt(ms + NORM_EPS)).T
        sl = slice(hh * hd2, (hh + 1) * hd2)
        o_ref[:, sl] = (on * gn_ref[:, sl] * (1.0 - lam_init)).astype(BF16)


def _attention_prompt(p, qt, vt, lam_rows, da_gn_row, n_seq, seq_len, blk, lam_init):
    nb = seq_len // blk
    n_hp = DA_HEADS // 2
    hd2 = 2 * DA_HEAD_DIM
    kcol0 = COL_KD * COL_W // hd2
    ek, bq = _alibi_operands(blk)

    smem = pl.BlockSpec(memory_space=pltpu.SMEM)
    in_specs = [
        smem, smem,
        pl.BlockSpec((1, hd2, blk), lambda b, j, q: (b * nb + q, j, 0)),
        pl.BlockSpec((1, hd2, blk), lambda b, j, q: (b * nb + q, n_hp + j, 0)),
        pl.BlockSpec((seq_len, hd2), lambda b, j, q: (b, kcol0 + j)),
        pl.BlockSpec((seq_len, hd2), lambda b, j, q: (b, kcol0 + n_hp + j)),
        pl.BlockSpec((nb, 2 * hd2, blk), lambda b, j, q: (b, j, 0)),
        pl.BlockSpec((blk, hd2), lambda b, j, q: (0, 0)),
        pl.BlockSpec((hd2, blk), lambda b, j, q: (0, 0)),
    ] + [pl.BlockSpec((1, DA_HEAD_DIM), lambda b, j, q: (0, 0))] * 4 + [
        pl.BlockSpec((1, 2 * hd2), lambda b, j, q: (0, j)),
    ]
    return pl.pallas_call(
        functools.partial(_attn_kernel, blk=blk, lam_init=lam_init),
        grid=(n_seq, n_hp, nb), in_specs=in_specs,
        out_specs=pl.BlockSpec((blk, 2 * hd2), lambda b, j, q: (b * nb + q, j)),
        out_shape=jax.ShapeDtypeStruct((n_seq * seq_len, COL_W), BF16),
        scratch_shapes=[pltpu.VMEM((4, 2 * hd2, blk), BF16),
                        pltpu.VMEM((4, 1, blk), F32),
                        pltpu.VMEM((4, 1, blk), F32),
                        pltpu.VMEM((4, hd2, blk), F32),
                        pltpu.VMEM((2, 4, blk, blk), F32),
                        pltpu.VMEM((2, 4, 1, blk), F32)],
        compiler_params=_cparams(3), name="diff_attn_prompt",
    )(jnp.asarray(_SLOPES_LOG2), jnp.asarray(_SLOPES), qt, qt, p, p, vt, ek, bq,
      *lam_rows, da_gn_row)


def _sattn_kernel(slopes_ref, q_ref, kn_ref, vn_ref, kc_ref, vc_ref,
                  lq1, lk1, lq2, lk2, gn_ref, o_ref, *, L, P, lam_init):
    lam = _lambda_value(lq1, lk1, lq2, lk2, lam_init)
    hd2 = 2 * DA_HEAD_DIM
    lane = lax.broadcasted_iota(jnp.int32, (L, hd2), 1)
    qpos_c = P + lax.broadcasted_iota(jnp.int32, (L, P), 0)
    kpos_c = lax.broadcasted_iota(jnp.int32, (L, P), 1)
    qpos_n = P + lax.broadcasted_iota(jnp.int32, (L, L), 0)
    kpos_n = P + lax.broadcasted_iota(jnp.int32, (L, L), 1)
    dist_c = jnp.abs(qpos_c - kpos_c).astype(F32)
    dist_n = jnp.abs(qpos_n - kpos_n).astype(F32)
    ok_c = (kpos_c // CHUNK) <= (qpos_c // CHUNK)
    ok_n = (kpos_n // CHUNK) <= (qpos_n // CHUNK)
    nt = (((1,), (1,)), ((), ()))
    for pr in range(DA_HEADS // 2):
        for hh in range(2):
            h = 2 * pr + hh
            slope = slopes_ref[h]
            probs = []
            for c in range(2):
                sl = slice((c * (DA_HEADS // 2) + pr) * hd2, (c * (DA_HEADS // 2) + pr + 1) * hd2)
                q128 = q_ref[:, sl].astype(F32) * _Q_SCALE_LOG2
                keep = (lane < DA_HEAD_DIM) if hh == 0 else (lane >= DA_HEAD_DIM)
                qm = jnp.where(keep, q128, 0.0).astype(BF16)
                kc = kc_ref[0, :, sl].astype(BF16)
                kn = kn_ref[:, sl]
                s_c = lax.dot_general(qm, kc, nt, preferred_element_type=F32)
                s_n = lax.dot_general(qm, kn, nt, preferred_element_type=F32)
                s_c = jnp.where(ok_c, s_c - slope * dist_c, _NEG)
                s_n = jnp.where(ok_n, s_n - slope * dist_n, _NEG)
                m = jnp.maximum(jnp.max(s_c, axis=-1, keepdims=True),
                                jnp.max(s_n, axis=-1, keepdims=True))
                p_c = jnp.exp2(s_c - m)
                p_n = jnp.exp2(s_n - m)
                inv = 1.0 / (jnp.sum(p_c, axis=-1, keepdims=True)
                             + jnp.sum(p_n, axis=-1, keepdims=True))
                probs.append((p_c * inv, p_n * inv))
            a_c = (probs[0][0] - lam * probs[1][0]).astype(BF16)
            a_n = (probs[0][1] - lam * probs[1][1]).astype(BF16)
            vsl = slice(h * hd2, (h + 1) * hd2)
            o = (jnp.dot(a_c, vc_ref[0, :, vsl].astype(BF16), preferred_element_type=F32)
                 + jnp.dot(a_n, vn_ref[:, vsl], preferred_element_type=F32))
            ms = jnp.mean(o * o, axis=-1, keepdims=True)
            on = o * lax.rsqrt(ms + NORM_EPS) * gn_ref[:, vsl] * (1.0 - lam_init)
            o_ref[:, vsl] = on.astype(BF16)


def _attention_sample(p, cache_k, cache_v, lam_rows, da_gn_row, n_seq, L, lam_init):
    P = cache_k.shape[1]
    smem = pl.BlockSpec(memory_space=pltpu.SMEM)
    in_specs = [
        smem,
        pl.BlockSpec((L, COL_W), lambda s: (s, COL_QD)),
        pl.BlockSpec((L, COL_W), lambda s: (s, COL_KD)),
        pl.BlockSpec((L, COL_W), lambda s: (s, COL_VD)),
        pl.BlockSpec((1, P, COL_W), lambda s: (s, 0, 0)),
        pl.BlockSpec((1, P, COL_W), lambda s: (s, 0, 0)),
    ] + [pl.BlockSpec((1, DA_HEAD_DIM), lambda s: (0, 0))] * 4 + [
        pl.BlockSpec((1, COL_W), lambda s: (0, 0)),
    ]
    return pl.pallas_call(
        functools.partial(_sattn_kernel, L=L, P=P, lam_init=lam_init),
        grid=(n_seq,), in_specs=in_specs,
        out_specs=pl.BlockSpec((L, COL_W), lambda s: (s, 0)),
        out_shape=jax.ShapeDtypeStruct((n_seq * L, COL_W), BF16),
        compiler_params=_cparams(1), name="diff_attn_sample",
    )(jnp.asarray(_SLOPES_LOG2), p, p, p, cache_k, cache_v, *lam_rows, da_gn_row)


def _merge_kernel(x_ref, yr_ref, yd_ref, gr0, gr1, gd0, gd1, wr_ref, wd_ref, wo_ref, gm_ref,
                  x1_ref, h2_ref):
    yr = jnp.dot(yr_ref[...], wr_ref[...], preferred_element_type=F32)
    yd = jnp.dot(yd_ref[...], wd_ref[...], preferred_element_type=F32)
    gr = jnp.concatenate([gr0[...], gr1[...]], axis=1).astype(F32)
    gd = jnp.concatenate([gd0[...], gd1[...]], axis=1).astype(F32)
    m = jax.nn.sigmoid(gr) * yr + jax.nn.sigmoid(gd) * yd
    x1 = x_ref[...] + jnp.dot(m.astype(BF16), wo_ref[...], preferred_element_type=F32)
    x1_ref[...] = x1
    ms = jnp.mean(x1 * x1, axis=-1, keepdims=True)
    h2_ref[...] = (x1 * lax.rsqrt(ms + NORM_EPS) * gm_ref[...]).astype(BF16)


def _merge(x2d, yr_in, yd_in, p, w_ret_o, w_da_o, w_out, g_mlp_row, tm):
    T = x2d.shape[0]

    def whole(shape):
        return pl.BlockSpec(shape, lambda i: (0, 0), pipeline_mode=pl.Buffered(1))

    def pcol(t):
        return pl.BlockSpec((tm, COL_W), lambda i: (i, t))

    in_specs = [
        pl.BlockSpec((tm, D_MODEL), lambda i: (i, 0)),
        pl.BlockSpec((tm, COL_W), lambda i: (i, 0)),
        pl.BlockSpec((tm, COL_W), lambda i: (i, 0)),
        pcol(COL_GR), pcol(COL_GR + 1), pcol(COL_GD), pcol(COL_GD + 1),
        whole((COL_W, D_MODEL)), whole((COL_W, D_MODEL)), whole((D_MODEL, D_MODEL)),
        pl.BlockSpec((1, D_MODEL), lambda i: (0, 0)),
    ]
    return pl.pallas_call(
        _merge_kernel, grid=(T // tm,), in_specs=in_specs,
        out_specs=[pl.BlockSpec((tm, D_MODEL), lambda i: (i, 0)),
                   pl.BlockSpec((tm, D_MODEL), lambda i: (i, 0))],
        out_shape=[jax.ShapeDtypeStruct((T, D_MODEL), F32),
                   jax.ShapeDtypeStruct((T, D_MODEL), BF16)],
        compiler_params=_cparams(1), name="merge",
    )(x2d, yr_in, yd_in, p, p, p, p, w_ret_o, w_da_o, w_out, g_mlp_row)


def _mlp_kernel(x1_ref, h2_ref, wu_ref, wd_ref, y_ref):
    f = pl.program_id(1)
    u = jnp.dot(h2_ref[...], wu_ref[...], preferred_element_type=F32)
    r = jnp.square(jnp.maximum(u, 0.0)).astype(BF16)
    contrib = jnp.dot(r, wd_ref[...], preferred_element_type=F32)

    @pl.when(f == 0)
    def _():
        y_ref[...] = x1_ref[...] + contrib

    @pl.when(f > 0)
    def _():
        y_ref[...] += contrib


def _mlp(x1, h2, w_up, w_down, tm, tf):
    T = x1.shape[0]
    return pl.pallas_call(
        _mlp_kernel, grid=(T // tm, D_FF // tf),
        in_specs=[pl.BlockSpec((tm, D_MODEL), lambda i, f: (i, 0)),
                  pl.BlockSpec((tm, D_MODEL), lambda i, f: (i, 0)),
                  pl.BlockSpec((D_MODEL, tf), lambda i, f: (0, f)),
                  pl.BlockSpec((tf, D_MODEL), lambda i, f: (f, 0))],
        out_specs=pl.BlockSpec((tm, D_MODEL), lambda i, f: (i, 0)),
        out_shape=jax.ShapeDtypeStruct((T, D_MODEL), F32),
        compiler_params=_cparams(2), name="mlp",
    )(x1, h2, w_up, w_down)


def _rope_tables(pos):
    half = RET_DK // 2
    inv = jnp.exp(-math.log(ROPE_BASE) * jnp.arange(half, dtype=F32) / half)
    ang = pos[:, None] * inv[None, :]
    cos = jnp.cos(ang)
    sin = jnp.sin(ang)
    return jnp.concatenate([cos, cos], axis=1), jnp.concatenate([-sin, sin], axis=1)


def _block_diag_mean():
    r = np.arange(V7X_MXU_DIM) // DA_HEAD_DIM
    return jnp.asarray((r[:, None] == r[None, :]).astype(np.float32) / DA_HEAD_DIM, BF16)


def kernel(x_prompt, x_sample, cache_da_k, cache_da_v, state_ret, g_mix, w_in, ret_gn, qn_g, kn_g,
           lambda_q1, lambda_k1, lambda_q2, lambda_k2, da_gn, w_ret_o, w_da_o, w_out, g_mlp,
           w_up, w_down):
    B, S, _ = x_prompt.shape
    DB, L, _ = x_sample.shape
    P = cache_da_k.shape[2]
    depth = w_in.shape[0]
    assert S % TOKEN_TILE == 0 and S % RET_CHUNK_PROMPT == 0 and L % 16 == 0

    tm_p = TOKEN_TILE
    tm_s = _pick_tile(DB * L, TOKEN_TILE)
    assert tm_s % L == 0 or L % tm_s == 0
    cos_p, sin_p = _rope_tables(jnp.arange(S, dtype=F32))
    cos_s, sin_s = _rope_tables(P + jnp.arange(L, dtype=F32))
    if tm_s > L:
        cos_s = jnp.tile(cos_s, (tm_s // L, 1))
        sin_s = jnp.tile(sin_s, (tm_s // L, 1))
    bd = _block_diag_mean()
    n_heads_qk = COL_W // DA_HEAD_DIM

    xp = x_prompt.reshape(B * S, D_MODEL)
    xs = x_sample.reshape(DB * L, D_MODEL)
    kp_l, vp_l, rp_l, ks_l, vs_l, rs_l = [], [], [], [], [], []
    for l in range(depth):
        lam_init = 0.8 - 0.6 * math.exp(-0.3 * l)
        w_in_bf = w_in[l].astype(BF16)
        w_ro, w_do, w_o = w_ret_o[l].astype(BF16), w_da_o[l].astype(BF16), w_out[l].astype(BF16)
        w_u, w_d = w_up[l].astype(BF16), w_down[l].astype(BF16)
        g_mix_row = g_mix[l].reshape(1, D_MODEL)
        g_mlp_row = g_mlp[l].reshape(1, D_MODEL)
        qg16 = jnp.tile(qn_g[l], n_heads_qk).reshape(1, COL_W)
        kg16 = jnp.tile(kn_g[l], n_heads_qk).reshape(1, COL_W)
        ret_gn_row = ret_gn[l].reshape(1, COL_W)
        da_gn_row = da_gn[l].reshape(1, COL_W)
        lam_rows = [a[l].astype(F32).reshape(1, DA_HEAD_DIM)
                    for a in (lambda_q1, lambda_k1, lambda_q2, lambda_k2)]

        p, nk, nv, qt, vt = _inproj(xp, g_mix_row, w_in_bf, cos_p, sin_p, qg16, kg16, bd,
                                    tm_p, True)
        yr_in, s_fin = _retention(p, ret_gn_row, B, S, RET_CHUNK_PROMPT, None)
        yd_in = _attention_prompt(p, qt, vt, lam_rows, da_gn_row, B, S, TOKEN_TILE, lam_init)
        x1, h2 = _merge(xp, yr_in, yd_in, p, w_ro, w_do, w_o, g_mlp_row, tm_p)
        xp = _mlp(x1, h2, w_u, w_d, tm_p, COL_W)
        kp_l.append(nk.reshape(B, S, 2 * DA_HEADS, DA_HEAD_DIM))
        vp_l.append(nv.reshape(B, S, DA_HEADS, 2 * DA_HEAD_DIM))
        rp_l.append(s_fin)

        p, nk, nv = _inproj(xs, g_mix_row, w_in_bf, cos_s, sin_s, qg16, kg16, bd, tm_s, False)
        yr_in, s_new = _retention(p, ret_gn_row, DB, L, L, state_ret[l].astype(F32))
        yd_in = _attention_sample(p, cache_da_k[l].reshape(DB, P, COL_W),
                                  cache_da_v[l].reshape(DB, P, COL_W),
                                  lam_rows, da_gn_row, DB, L, lam_init)
        x1, h2 = _merge(xs, yr_in, yd_in, p, w_ro, w_do, w_o, g_mlp_row, tm_s)
        xs = _mlp(x1, h2, w_u, w_d, tm_s, COL_W)
        ks_l.append(nk.reshape(DB, L, 2 * DA_HEADS, DA_HEAD_DIM))
        vs_l.append(nv.reshape(DB, L, DA_HEADS, 2 * DA_HEAD_DIM))
        rs_l.append(s_new)

    return (xp.reshape(B, S, D_MODEL), xs.reshape(DB, L, D_MODEL),
            jnp.stack(kp_l, 0), jnp.stack(vp_l, 0), jnp.stack(rp_l, 0),
            jnp.stack(ks_l, 0), jnp.stack(vs_l, 0), jnp.stack(rs_l, 0))
```

```python
import functools
import math

import numpy as np
import jax
import jax.numpy as jnp
from jax import lax
from jax.experimental import pallas as pl
from jax.experimental.pallas import tpu as pltpu

D_MODEL = 2048
CHUNK = 64
RET_HEADS = 8
RET_DK = 128
RET_DV = 128
DA_HEADS = 8
DA_HEAD_DIM = 64
D_FF = 4 * D_MODEL
ROPE_BASE = 10000.0
NORM_EPS = 1e-6
LOG2E = 1.4426950408889634

V7X_LANES = 128
V7X_MXU_DIM = 256
V7X_VMEM_LIMIT_BYTES = 56 * 1024 * 1024

COL_W = 1024
N_COL_TILES = 11
COL_QR, COL_KR, COL_VR, COL_ZR, COL_QD, COL_KD, COL_VD = 0, 1, 2, 3, 4, 5, 6
COL_GR, COL_GD = 7, 9

TOKEN_TILE = 512
RET_CHUNK_PROMPT = 256
MLP_FF_TILE = 2048

F32 = jnp.float32
BF16 = jnp.bfloat16

_LOG_G = [math.log1p(-(2.0 ** (-5.0 - h))) for h in range(RET_HEADS)]
_SLOPES = np.asarray([2.0 ** (-8.0 * (h + 1) / DA_HEADS) for h in range(DA_HEADS)], np.float32)
_SLOPES_LOG2 = (_SLOPES * np.float32(LOG2E)).astype(np.float32)
_Q_SCALE_LOG2 = (DA_HEAD_DIM ** -0.5) * LOG2E
_NEG = -1e30


def _cparams(n_grid):
    return pltpu.CompilerParams(
        dimension_semantics=("arbitrary",) * n_grid,
        vmem_limit_bytes=V7X_VMEM_LIMIT_BYTES)


def _pick_tile(n, cap):
    t = cap
    while n % t:
        t //= 2
    return t


def _inproj_kernel(x_ref, g_ref, w_ref, cos_ref, sin_ref, qg_ref, kg_ref, bd_ref,
                   p_ref, nk_ref, nv_ref, *rest, emit_t):
    if emit_t:
        qt_ref, vt_ref, h_scr = rest
    else:
        (h_scr,) = rest
    j = pl.program_id(1)

    @pl.when(j == 0)
    def _():
        x = x_ref[...]
        ms = jnp.mean(x * x, axis=-1, keepdims=True)
        h_scr[...] = (x * lax.rsqrt(ms + NORM_EPS) * g_ref[...]).astype(BF16)

    acc = jnp.dot(h_scr[...], w_ref[...], preferred_element_type=F32)

    def rope(scale):
        cos = cos_ref[...]
        sin = sin_ref[...]
        for hh in range(RET_HEADS):
            sl = slice(hh * RET_DK, (hh + 1) * RET_DK)
            a = acc[:, sl]
            o = a * cos + pltpu.roll(a, RET_DK // 2, axis=1) * sin
            if scale != 1.0:
                o = o * scale
            p_ref[:, sl] = o.astype(BF16)

    def head_norm(gain_row):
        sq = (acc * acc).astype(BF16)
        parts = [
            jnp.dot(sq[:, c * V7X_MXU_DIM:(c + 1) * V7X_MXU_DIM], bd_ref[...],
                    preferred_element_type=F32)
            for c in range(COL_W // V7X_MXU_DIM)
        ]
        ms = jnp.concatenate(parts, axis=1)
        return acc * lax.rsqrt(ms + NORM_EPS) * gain_row

    @pl.when(j == COL_QR)
    def _():
        rope(1.0)

    @pl.when(j == COL_KR)
    def _():
        rope(RET_DK ** -0.5)

    @pl.when((j == COL_VR) | (j == COL_ZR) | (j >= COL_GR))
    def _():
        p_ref[...] = acc.astype(BF16)

    @pl.when(j == COL_QD)
    def _():
        n = head_norm(qg_ref[...])
        p_ref[...] = n.astype(BF16)
        if emit_t:
            qt_ref[0] = (n * _Q_SCALE_LOG2).T.astype(BF16)

    @pl.when(j == COL_KD)
    def _():
        n = head_norm(kg_ref[...])
        if emit_t:
            nk_ref[0] = n.T
        else:
            nk_ref[...] = n
        p_ref[...] = n.astype(BF16)

    @pl.when(j == COL_VD)
    def _():
        tm = acc.shape[0]
        for h in range(DA_HEADS):
            nv_ref[pl.ds(h, tm, stride=DA_HEADS), :] = acc[:, h * RET_DV:(h + 1) * RET_DV]
        p_ref[...] = acc.astype(BF16)
        if emit_t:
            vt_ref[0] = acc.T.astype(BF16)


def _inproj(x2d, g_mix, w_bf, cos_tab, sin_tab, qg16, kg16, bd, tm, seq_len):
    emit_t = seq_len is not None
    T = x2d.shape[0]
    n_rope = cos_tab.shape[0] // tm
    grid = (T // tm, N_COL_TILES)
    in_specs = [
        pl.BlockSpec((tm, D_MODEL), lambda i, j: (i, 0)),
        pl.BlockSpec((1, D_MODEL), lambda i, j: (0, 0)),
        pl.BlockSpec((D_MODEL, COL_W), lambda i, j: (0, j)),
        pl.BlockSpec((tm, RET_DK), lambda i, j: (i % n_rope, 0)),
        pl.BlockSpec((tm, RET_DK), lambda i, j: (i % n_rope, 0)),
        pl.BlockSpec((1, COL_W), lambda i, j: (0, 0)),
        pl.BlockSpec((1, COL_W), lambda i, j: (0, 0)),
        pl.BlockSpec((V7X_MXU_DIM, V7X_MXU_DIM), lambda i, j: (0, 0)),
    ]
    if emit_t:
        spb = seq_len // tm
        nk_shape = jax.ShapeDtypeStruct((T // seq_len, COL_W, seq_len), F32)
        nk_spec = pl.BlockSpec((1, COL_W, tm), lambda i, j: (i // spb, 0, i % spb))
    else:
        nk_shape = jax.ShapeDtypeStruct((T, COL_W), F32)
        nk_spec = pl.BlockSpec((tm, COL_W), lambda i, j: (i, 0))
    out_shape = [
        jax.ShapeDtypeStruct((T, N_COL_TILES * COL_W), BF16),
        nk_shape,
        jax.ShapeDtypeStruct((T * DA_HEADS, RET_DV), F32),
    ]
    out_specs = [
        pl.BlockSpec((tm, COL_W), lambda i, j: (i, j)),
        nk_spec,
        pl.BlockSpec((tm * DA_HEADS, RET_DV), lambda i, j: (i, 0)),
    ]
    if emit_t:
        out_shape += [jax.ShapeDtypeStruct((T // tm, COL_W, tm), BF16)] * 2
        out_specs += [pl.BlockSpec((1, COL_W, tm), lambda i, j: (i, 0, 0))] * 2
    return pl.pallas_call(
        functools.partial(_inproj_kernel, emit_t=emit_t),
        grid=grid, in_specs=in_specs, out_specs=out_specs, out_shape=out_shape,
        scratch_shapes=[pltpu.VMEM((tm, D_MODEL), BF16)],
        compiler_params=_cparams(2), name="inproj_t" if emit_t else "inproj",
    )(x2d, g_mix, w_bf, cos_tab, sin_tab, qg16, kg16, bd)


def _ret_kernel(q_ref, k_ref, v_ref, z_ref, gn_ref, *rest, C, has_init):
    if has_init:
        s0_ref, y_ref, s_ref, dm_scr, rd_scr, kd_scr = rest
    else:
        y_ref, s_ref, dm_scr, rd_scr, kd_scr = rest
    b = pl.program_id(0)
    c = pl.program_id(1)

    @pl.when((b == 0) & (c == 0))
    def _():
        li = lax.broadcasted_iota(jnp.int32, (C, C), 0)
        mi = lax.broadcasted_iota(jnp.int32, (C, C), 1)
        causal = li >= mi
        diff = jnp.where(causal, (li - mi).astype(F32), 0.0)
        row = lax.broadcasted_iota(jnp.int32, (C, RET_DK), 0).astype(F32)
        for h in range(RET_HEADS):
            dm_scr[h] = jnp.where(causal, jnp.exp(diff * _LOG_G[h]), 0.0)
            rd_scr[h] = jnp.exp((row + 1.0) * _LOG_G[h])
            kd_scr[h] = jnp.exp((C - 1.0 - row) * _LOG_G[h])

    @pl.when(c == 0)
    def _():
        if has_init:
            s_ref[...] = s0_ref[...]
        else:
            s_ref[...] = jnp.zeros(s_ref.shape, F32)

    for h in range(RET_HEADS):
        sl = slice(h * RET_DK, (h + 1) * RET_DK)
        q = q_ref[:, sl]
        k = k_ref[:, sl]
        v = v_ref[:, sl]
        state = s_ref[0, h]
        sc = lax.dot_general(q, k, (((1,), (1,)), ((), ())), preferred_element_type=F32)
        sc = sc * dm_scr[h]
        intra = jnp.dot(sc.astype(BF16), v, preferred_element_type=F32)
        inter = jnp.dot(q, state.astype(BF16), preferred_element_type=F32) * rd_scr[h]
        o = intra + inter
        kw = (k.astype(F32) * kd_scr[h]).astype(BF16)
        upd = lax.dot_general(kw, v, (((0,), (0,)), ((), ())), preferred_element_type=F32)
        s_ref[0, h] = state * math.exp(C * _LOG_G[h]) + upd
        ms = jnp.mean(o * o, axis=-1, keepdims=True)
        z = z_ref[:, sl].astype(F32)
        y = o * lax.rsqrt(ms + NORM_EPS) * gn_ref[:, sl] * (z * jax.nn.sigmoid(z))
        y_ref[:, sl] = y.astype(BF16)


def _retention(p, ret_gn_row, n_seq, seq_len, C, init_state):
    n_chunks = seq_len // C
    has_init = init_state is not None

    def col(t):
        return pl.BlockSpec((C, COL_W), lambda b, c: (b * n_chunks + c, t))

    in_specs = [col(COL_QR), col(COL_KR), col(COL_VR), col(COL_ZR),
                pl.BlockSpec((1, COL_W), lambda b, c: (0, 0))]
    args = [p, p, p, p, ret_gn_row]
    state_spec = pl.BlockSpec((1, RET_HEADS, RET_DK, RET_DV), lambda b, c: (b, 0, 0, 0))
    if has_init:
        in_specs.append(state_spec)
        args.append(init_state)
    return pl.pallas_call(
        functools.partial(_ret_kernel, C=C, has_init=has_init),
        grid=(n_seq, n_chunks), in_specs=in_specs,
        out_specs=[pl.BlockSpec((C, COL_W), lambda b, c: (b * n_chunks + c, 0)), state_spec],
        out_shape=[jax.ShapeDtypeStruct((n_seq * seq_len, COL_W), BF16),
                   jax.ShapeDtypeStruct((n_seq, RET_HEADS, RET_DK, RET_DV), F32)],
        scratch_shapes=[pltpu.VMEM((RET_HEADS, C, C), F32),
                        pltpu.VMEM((RET_HEADS, C, RET_DK), F32),
                        pltpu.VMEM((RET_HEADS, C, RET_DK), F32)],
        compiler_params=_cparams(2), name="retention_init" if has_init else "retention",
    )(*args)


def _lambda_value(lq1, lk1, lq2, lk2, lam_init):
    a = jnp.sum(lq1[...] * lk1[...], axis=-1, keepdims=True)
    b = jnp.sum(lq2[...] * lk2[...], axis=-1, keepdims=True)
    return jnp.exp(a) - jnp.exp(b) + lam_init


def _bf16_split3(x):
    out, r = [], np.float64(x)
    for _ in range(3):
        c = np.float64(np.asarray(r, np.float32).astype(jnp.bfloat16).astype(np.float32))
        out.append(float(c))
        r = r - c
    return out


def _alibi_operands(blk):
    c = _bf16_split3(LOG2E)
    pos = np.arange(blk)
    ek = np.zeros((blk, 2 * DA_HEAD_DIM), np.float32)
    bq = np.zeros((2 * DA_HEAD_DIM, blk), np.float32)
    for t in range(3):
        ek[:, t] = pos >> 4
        ek[:, 3 + t] = pos & 15
        ek[:, 6 + t] = -16.0 * c[t]
        ek[:, 9 + t] = -c[t]
        bq[t, :] = 16.0 * c[t]
        bq[3 + t, :] = c[t]
        bq[6 + t, :] = pos >> 4
        bq[9 + t, :] = pos & 15
    return jnp.asarray(ek, BF16), jnp.asarray(bq, BF16)


def _attn_kernel(slog2_ref, spow2_ref, qt1_ref, qt2_ref, k1_ref, k2_ref, vt_ref, ek_ref, bq_ref,
                 lq1, lk1, lq2, lk2, gn_ref, o_ref,
                 qx_scr, m_scr, l_scr, acc_scr, s_scr, mb_scr, *, blk, lam_init):
    j = pl.program_id(1)
    qi = pl.program_id(2)
    half = DA_HEAD_DIM
    hd2 = 2 * DA_HEAD_DIM
    k_refs = (k1_ref, k2_ref)

    zeros = jnp.zeros((half, blk), BF16)
    for hh in range(2):
        bias_rows = (bq_ref[...].astype(F32) * spow2_ref[2 * j + hh]).astype(BF16)
        for c, qref in enumerate((qt1_ref, qt2_ref)):
            idx = 2 * c + hh
            if hh == 0:
                qx_scr[idx, :half, :] = qref[0, :half, :]
                qx_scr[idx, half:hd2, :] = zeros
            else:
                qx_scr[idx, :half, :] = zeros
                qx_scr[idx, half:hd2, :] = qref[0, half:, :]
            qx_scr[idx, hd2:, :] = bias_rows
    m_scr[...] = jnp.full(m_scr.shape, _NEG, F32)
    l_scr[...] = jnp.zeros(l_scr.shape, F32)
    acc_scr[...] = jnp.zeros(acc_scr.shape, F32)

    def score_diag(par):
        kk = lax.broadcasted_iota(jnp.int32, (blk, blk), 0)
        qq = lax.broadcasted_iota(jnp.int32, (blk, blk), 1)
        allowed = (kk // CHUNK) <= (qq // CHUNK)
        dist = jnp.abs(kk - qq).astype(F32)
        row0 = pl.multiple_of(qi * blk, blk)
        for c in range(2):
            kc = k_refs[c][pl.ds(row0, blk), :]
            for hh in range(2):
                idx = 2 * c + hh
                s = jnp.dot(kc, qx_scr[idx, :hd2, :], preferred_element_type=F32)
                s = jnp.where(allowed, s - slog2_ref[2 * j + hh] * dist, _NEG)
                s_scr[par, idx] = s
                mb_scr[par, idx] = jnp.max(s, axis=0, keepdims=True)

    def score_off(ki, par):
        row0 = pl.multiple_of(ki * blk, blk)
        for c in range(2):
            kx = jnp.concatenate([k_refs[c][pl.ds(row0, blk), :], ek_ref[...]], axis=1)
            for hh in range(2):
                idx = 2 * c + hh
                s = jnp.dot(kx, qx_scr[idx], preferred_element_type=F32)
                s_scr[par, idx] = s
                mb_scr[par, idx] = jnp.max(s, axis=0, keepdims=True)

    def consume(kb, par):
        dblk = ((qi - kb) * blk).astype(F32)
        for c in range(2):
            for hh in range(2):
                idx = 2 * c + hh
                shift = slog2_ref[2 * j + hh] * dblk
                m_old = m_scr[idx]
                m_new = jnp.maximum(m_old, mb_scr[par, idx] - shift)
                alpha = jnp.exp2(m_old - m_new)
                p = jnp.exp2(s_scr[par, idx] - (m_new + shift))
                l_scr[idx] = alpha * l_scr[idx] + jnp.sum(p, axis=0, keepdims=True)
                vt_h = vt_ref[kb, hh * hd2:(hh + 1) * hd2, :]
                acc_scr[idx] = alpha * acc_scr[idx] + jnp.dot(
                    vt_h, p.astype(BF16), preferred_element_type=F32)
                m_scr[idx] = m_new

    score_diag(0)

    def pair_body(u, carry):
        t0 = 2 * u
        score_off(t0, 1)
        consume(jnp.where(t0 == 0, qi, t0 - 1), 0)
        score_off(t0 + 1, 0)
        consume(t0, 1)
        return carry

    lax.fori_loop(0, qi // 2, pair_body, 0)

    @pl.when(qi % 2 == 1)
    def _():
        t0 = qi - 1
        score_off(t0, 1)
        consume(jnp.where(t0 == 0, qi, t0 - 1), 0)
        consume(t0, 1)

    @pl.when(qi % 2 == 0)
    def _():
        consume(jnp.where(qi == 0, qi, qi - 1), 0)

    lam = _lambda_value(lq1, lk1, lq2, lk2, lam_init)
    for hh in range(2):
        o = acc_scr[hh] / l_scr[hh] - lam * (acc_scr[2 + hh] / l_scr[2 + hh])
        ms = jnp.mean(o * o, axis=0, keepdims=True)
        on = (o * lax.rsqrt(ms + NORM_EPS)).T
        sl = slice(hh * hd2, (hh + 1) * hd2)
        o_ref[:, sl] = (on * gn_ref[:, sl] * (1.0 - lam_init)).astype(BF16)


def _attention_prompt(p, qt, vt, lam_rows, da_gn_row, n_seq, seq_len, blk, lam_init):
    nb = seq_len // blk
    n_hp = DA_HEADS // 2
    hd2 = 2 * DA_HEAD_DIM
    kcol0 = COL_KD * COL_W // hd2
    ek, bq = _alibi_operands(blk)

    smem = pl.BlockSpec(memory_space=pltpu.SMEM)
    in_specs = [
        smem, smem,
        pl.BlockSpec((1, hd2, blk), lambda b, j, q: (b * nb + q, j, 0)),
        pl.BlockSpec((1, hd2, blk), lambda b, j, q: (b * nb + q, n_hp + j, 0)),
        pl.BlockSpec((seq_len, hd2), lambda b, j, q: (b, kcol0 + j)),
        pl.BlockSpec((seq_len, hd2), lambda b, j, q: (b, kcol0 + n_hp + j)),
        pl.BlockSpec((nb, 2 * hd2, blk), lambda b, j, q: (b, j, 0)),
        pl.BlockSpec((blk, hd2), lambda b, j, q: (0, 0)),
        pl.BlockSpec((hd2, blk), lambda b, j, q: (0, 0)),
    ] + [pl.BlockSpec((1, DA_HEAD_DIM), lambda b, j, q: (0, 0))] * 4 + [
        pl.BlockSpec((1, 2 * hd2), lambda b, j, q: (0, j)),
    ]
    return pl.pallas_call(
        functools.partial(_attn_kernel, blk=blk, lam_init=lam_init),
        grid=(n_seq, n_hp, nb), in_specs=in_specs,
        out_specs=pl.BlockSpec((blk, 2 * hd2), lambda b, j, q: (b * nb + q, j)),
        out_shape=jax.ShapeDtypeStruct((n_seq * seq_len, COL_W), BF16),
        scratch_shapes=[pltpu.VMEM((4, 2 * hd2, blk), BF16),
                        pltpu.VMEM((4, 1, blk), F32),
                        pltpu.VMEM((4, 1, blk), F32),
                        pltpu.VMEM((4, hd2, blk), F32),
                        pltpu.VMEM((2, 4, blk, blk), F32),
                        pltpu.VMEM((2, 4, 1, blk), F32)],
        compiler_params=_cparams(3), name="diff_attn_prompt",
    )(jnp.asarray(_SLOPES_LOG2), jnp.asarray(_SLOPES), qt, qt, p, p, vt, ek, bq,
      *lam_rows, da_gn_row)


def _sattn_kernel(slopes_ref, q_ref, kn_ref, vn_ref, kc_ref, vc_ref,
                  lq1, lk1, lq2, lk2, gn_ref, o_ref, *, L, P, lam_init):
    lam = _lambda_value(lq1, lk1, lq2, lk2, lam_init)
    hd2 = 2 * DA_HEAD_DIM
    lane = lax.broadcasted_iota(jnp.int32, (L, hd2), 1)
    qpos_c = P + lax.broadcasted_iota(jnp.int32, (L, P), 0)
    kpos_c = lax.broadcasted_iota(jnp.int32, (L, P), 1)
    qpos_n = P + lax.broadcasted_iota(jnp.int32, (L, L), 0)
    kpos_n = P + lax.broadcasted_iota(jnp.int32, (L, L), 1)
    dist_c = jnp.abs(qpos_c - kpos_c).astype(F32)
    dist_n = jnp.abs(qpos_n - kpos_n).astype(F32)
    ok_c = (kpos_c // CHUNK) <= (qpos_c // CHUNK)
    ok_n = (kpos_n // CHUNK) <= (qpos_n // CHUNK)
    nt = (((1,), (1,)), ((), ()))
    for pr in range(DA_HEADS // 2):
        for hh in range(2):
            h = 2 * pr + hh
            slope = slopes_ref[h]
            probs = []
            for c in range(2):
                sl = slice((c * (DA_HEADS // 2) + pr) * hd2, (c * (DA_HEADS // 2) + pr + 1) * hd2)
                q128 = q_ref[:, sl].astype(F32) * _Q_SCALE_LOG2
                keep = (lane < DA_HEAD_DIM) if hh == 0 else (lane >= DA_HEAD_DIM)
                qm = jnp.where(keep, q128, 0.0).astype(BF16)
                kc_t = kc_ref[0, sl, :].astype(BF16)
                kn = kn_ref[:, sl]
                s_c = jnp.dot(qm, kc_t, preferred_element_type=F32)
                s_n = lax.dot_general(qm, kn, nt, preferred_element_type=F32)
                s_c = jnp.where(ok_c, s_c - slope * dist_c, _NEG)
                s_n = jnp.where(ok_n, s_n - slope * dist_n, _NEG)
                m = jnp.maximum(jnp.max(s_c, axis=-1, keepdims=True),
                                jnp.max(s_n, axis=-1, keepdims=True))
                p_c = jnp.exp2(s_c - m)
                p_n = jnp.exp2(s_n - m)
                inv = 1.0 / (jnp.sum(p_c, axis=-1, keepdims=True)
                             + jnp.sum(p_n, axis=-1, keepdims=True))
                probs.append((p_c * inv, p_n * inv))
            a_c = (probs[0][0] - lam * probs[1][0]).astype(BF16)
            a_n = (probs[0][1] - lam * probs[1][1]).astype(BF16)
            vsl = slice(h * hd2, (h + 1) * hd2)
            vc = vc_ref[0, pl.ds(h, P, stride=DA_HEADS), :].astype(BF16)
            o = (jnp.dot(a_c, vc, preferred_element_type=F32)
                 + jnp.dot(a_n, vn_ref[:, vsl], preferred_element_type=F32))
            ms = jnp.mean(o * o, axis=-1, keepdims=True)
            on = o * lax.rsqrt(ms + NORM_EPS) * gn_ref[:, vsl] * (1.0 - lam_init)
            o_ref[:, vsl] = on.astype(BF16)


def _attention_sample(p, cache_kt, cache_v, lam_rows, da_gn_row, n_seq, L, lam_init):
    P = cache_kt.shape[2]
    smem = pl.BlockSpec(memory_space=pltpu.SMEM)
    in_specs = [
        smem,
        pl.BlockSpec((L, COL_W), lambda s: (s, COL_QD)),
        pl.BlockSpec((L, COL_W), lambda s: (s, COL_KD)),
        pl.BlockSpec((L, COL_W), lambda s: (s, COL_VD)),
        pl.BlockSpec((1, COL_W, P), lambda s: (s, 0, 0)),
        pl.BlockSpec((1, P * DA_HEADS, RET_DV), lambda s: (s, 0, 0)),
    ] + [pl.BlockSpec((1, DA_HEAD_DIM), lambda s: (0, 0))] * 4 + [
        pl.BlockSpec((1, COL_W), lambda s: (0, 0)),
    ]
    return pl.pallas_call(
        functools.partial(_sattn_kernel, L=L, P=P, lam_init=lam_init),
        grid=(n_seq,), in_specs=in_specs,
        out_specs=pl.BlockSpec((L, COL_W), lambda s: (s, 0)),
        out_shape=jax.ShapeDtypeStruct((n_seq * L, COL_W), BF16),
        compiler_params=_cparams(1), name="diff_attn_sample",
    )(jnp.asarray(_SLOPES_LOG2), p, p, p, cache_kt, cache_v, *lam_rows, da_gn_row)


def _merge_kernel(x_ref, yr_ref, yd_ref, gr0, gr1, gd0, gd1, wr_ref, wd_ref, wo_ref, gm_ref,
                  x1_ref, h2_ref):
    yr = jnp.dot(yr_ref[...], wr_ref[...], preferred_element_type=F32)
    yd = jnp.dot(yd_ref[...], wd_ref[...], preferred_element_type=F32)
    gr = jnp.concatenate([gr0[...], gr1[...]], axis=1).astype(F32)
    gd = jnp.concatenate([gd0[...], gd1[...]], axis=1).astype(F32)
    m = jax.nn.sigmoid(gr) * yr + jax.nn.sigmoid(gd) * yd
    x1 = x_ref[...] + jnp.dot(m.astype(BF16), wo_ref[...], preferred_element_type=F32)
    x1_ref[...] = x1
    ms = jnp.mean(x1 * x1, axis=-1, keepdims=True)
    h2_ref[...] = (x1 * lax.rsqrt(ms + NORM_EPS) * gm_ref[...]).astype(BF16)


def _merge(x2d, yr_in, yd_in, p, w_ret_o, w_da_o, w_out, g_mlp_row, tm):
    T = x2d.shape[0]

    def whole(shape):
        return pl.BlockSpec(shape, lambda i: (0, 0), pipeline_mode=pl.Buffered(1))

    def pcol(t):
        return pl.BlockSpec((tm, COL_W), lambda i: (i, t))

    in_specs = [
        pl.BlockSpec((tm, D_MODEL), lambda i: (i, 0)),
        pl.BlockSpec((tm, COL_W), lambda i: (i, 0)),
        pl.BlockSpec((tm, COL_W), lambda i: (i, 0)),
        pcol(COL_GR), pcol(COL_GR + 1), pcol(COL_GD), pcol(COL_GD + 1),
        whole((COL_W, D_MODEL)), whole((COL_W, D_MODEL)), whole((D_MODEL, D_MODEL)),
        pl.BlockSpec((1, D_MODEL), lambda i: (0, 0)),
    ]
    return pl.pallas_call(
        _merge_kernel, grid=(T // tm,), in_specs=in_specs,
        out_specs=[pl.BlockSpec((tm, D_MODEL), lambda i: (i, 0)),
                   pl.BlockSpec((tm, D_MODEL), lambda i: (i, 0))],
        out_shape=[jax.ShapeDtypeStruct((T, D_MODEL), F32),
                   jax.ShapeDtypeStruct((T, D_MODEL), BF16)],
        compiler_params=_cparams(1), name="merge",
    )(x2d, yr_in, yd_in, p, p, p, p, w_ret_o, w_da_o, w_out, g_mlp_row)


def _mlp_kernel(x1_ref, h2_ref, wu_ref, wd_ref, y_ref):
    f = pl.program_id(1)
    u = jnp.dot(h2_ref[...], wu_ref[...], preferred_element_type=F32)
    r = jnp.square(jnp.maximum(u, 0.0)).astype(BF16)
    contrib = jnp.dot(r, wd_ref[...], preferred_element_type=F32)

    @pl.when(f == 0)
    def _():
        y_ref[...] = x1_ref[...] + contrib

    @pl.when(f > 0)
    def _():
        y_ref[...] += contrib


def _mlp(x1, h2, w_up, w_down, tm, tf):
    T = x1.shape[0]
    return pl.pallas_call(
        _mlp_kernel, grid=(T // tm, D_FF // tf),
        in_specs=[pl.BlockSpec((tm, D_MODEL), lambda i, f: (i, 0), pipeline_mode=pl.Buffered(1)),
                  pl.BlockSpec((tm, D_MODEL), lambda i, f: (i, 0), pipeline_mode=pl.Buffered(1)),
                  pl.BlockSpec((D_MODEL, tf), lambda i, f: (0, f)),
                  pl.BlockSpec((tf, D_MODEL), lambda i, f: (f, 0))],
        out_specs=pl.BlockSpec((tm, D_MODEL), lambda i, f: (i, 0)),
        out_shape=jax.ShapeDtypeStruct((T, D_MODEL), F32),
        compiler_params=_cparams(2), name="mlp",
    )(x1, h2, w_up, w_down)


def _rope_tables(pos):
    half = RET_DK // 2
    inv = jnp.exp(-math.log(ROPE_BASE) * jnp.arange(half, dtype=F32) / half)
    ang = pos[:, None] * inv[None, :]
    cos = jnp.cos(ang)
    sin = jnp.sin(ang)
    return jnp.concatenate([cos, cos], axis=1), jnp.concatenate([-sin, sin], axis=1)


def _block_diag_mean():
    r = np.arange(V7X_MXU_DIM) // DA_HEAD_DIM
    return jnp.asarray((r[:, None] == r[None, :]).astype(np.float32) / DA_HEAD_DIM, BF16)


def kernel(x_prompt, x_sample, cache_da_k, cache_da_v, state_ret, g_mix, w_in, ret_gn, qn_g, kn_g,
           lambda_q1, lambda_k1, lambda_q2, lambda_k2, da_gn, w_ret_o, w_da_o, w_out, g_mlp,
           w_up, w_down):
    B, S, _ = x_prompt.shape
    DB, L, _ = x_sample.shape
    P = cache_da_k.shape[2]
    depth = w_in.shape[0]
    assert S % TOKEN_TILE == 0 and S % RET_CHUNK_PROMPT == 0 and L % 16 == 0

    tm_p = TOKEN_TILE
    tm_s = _pick_tile(DB * L, TOKEN_TILE)
    assert tm_s % L == 0 or L % tm_s == 0
    cos_p, sin_p = _rope_tables(jnp.arange(S, dtype=F32))
    cos_s, sin_s = _rope_tables(P + jnp.arange(L, dtype=F32))
    if tm_s > L:
        cos_s = jnp.tile(cos_s, (tm_s // L, 1))
        sin_s = jnp.tile(sin_s, (tm_s // L, 1))
    bd = _block_diag_mean()
    n_heads_qk = COL_W // DA_HEAD_DIM

    xp = x_prompt.reshape(B * S, D_MODEL)
    xs = x_sample.reshape(DB * L, D_MODEL)
    kp_l, vp_l, rp_l, ks_l, vs_l, rs_l = [], [], [], [], [], []
    for l in range(depth):
        lam_init = 0.8 - 0.6 * math.exp(-0.3 * l)
        w_in_bf = w_in[l].astype(BF16)
        w_ro, w_do, w_o = w_ret_o[l].astype(BF16), w_da_o[l].astype(BF16), w_out[l].astype(BF16)
        w_u, w_d = w_up[l].astype(BF16), w_down[l].astype(BF16)
        g_mix_row = g_mix[l].reshape(1, D_MODEL)
        g_mlp_row = g_mlp[l].reshape(1, D_MODEL)
        qg16 = jnp.tile(qn_g[l], n_heads_qk).reshape(1, COL_W)
        kg16 = jnp.tile(kn_g[l], n_heads_qk).reshape(1, COL_W)
        ret_gn_row = ret_gn[l].reshape(1, COL_W)
        da_gn_row = da_gn[l].reshape(1, COL_W)
        lam_rows = [a[l].astype(F32).reshape(1, DA_HEAD_DIM)
                    for a in (lambda_q1, lambda_k1, lambda_q2, lambda_k2)]

        p, nkt, nv, qt, vt = _inproj(xp, g_mix_row, w_in_bf, cos_p, sin_p, qg16, kg16, bd,
                                     tm_p, S)
        yr_in, s_fin = _retention(p, ret_gn_row, B, S, RET_CHUNK_PROMPT, None)
        yd_in = _attention_prompt(p, qt, vt, lam_rows, da_gn_row, B, S, TOKEN_TILE, lam_init)
        x1, h2 = _merge(xp, yr_in, yd_in, p, w_ro, w_do, w_o, g_mlp_row, tm_p)
        xp = _mlp(x1, h2, w_u, w_d, tm_p, MLP_FF_TILE)
        nk = jnp.transpose(nkt.reshape(B, 2 * DA_HEADS, DA_HEAD_DIM, S), (0, 3, 1, 2))
        kp_l.append(nk)
        vp_l.append(nv.reshape(B, S, DA_HEADS, 2 * DA_HEAD_DIM))
        rp_l.append(s_fin)

        p, nk, nv = _inproj(xs, g_mix_row, w_in_bf, cos_s, sin_s, qg16, kg16, bd, tm_s, None)
        yr_in, s_new = _retention(p, ret_gn_row, DB, L, L, state_ret[l].astype(F32))
        cache_kt = jnp.transpose(cache_da_k[l].reshape(DB, P, COL_W), (0, 2, 1))
        yd_in = _attention_sample(p, cache_kt, cache_da_v[l].reshape(DB, P * DA_HEADS, RET_DV),
                                  lam_rows, da_gn_row, DB, L, lam_init)
        x1, h2 = _merge(xs, yr_in, yd_in, p, w_ro, w_do, w_o, g_mlp_row, tm_s)
        xs = _mlp(x1, h2, w_u, w_d, tm_s, MLP_FF_TILE)
        ks_l.append(nk.reshape(DB, L, 2 * DA_HEADS, DA_HEAD_DIM))
        vs_l.append(nv.reshape(DB, L, DA_HEADS, 2 * DA_HEAD_DIM))
        rs_l.append(s_new)

    return (xp.reshape(B, S, D_MODEL), xs.reshape(DB, L, D_MODEL),
            jnp.stack(kp_l, 0), jnp.stack(vp_l, 0), jnp.stack(rp_l, 0),
            jnp.stack(ks_l, 0), jnp.stack(vs_l, 0), jnp.stack(rs_l, 0))
```

```python
import functools
import math

import numpy as np
import jax
import jax.numpy as jnp
from jax import lax
from jax.experimental import pallas as pl
from jax.experimental.pallas import tpu as pltpu

D_MODEL = 2048
CHUNK = 64
RET_HEADS = 8
RET_DK = 128
RET_DV = 128
DA_HEADS = 8
DA_HEAD_DIM = 64
D_FF = 4 * D_MODEL
ROPE_BASE = 10000.0
NORM_EPS = 1e-6
LOG2E = 1.4426950408889634

V7X_LANES = 128
V7X_MXU_DIM = 256
V7X_VMEM_LIMIT_BYTES = 56 * 1024 * 1024

COL_W = 1024
N_COL_TILES = 11
COL_QR, COL_KR, COL_VR, COL_ZR, COL_QD, COL_KD, COL_VD = 0, 1, 2, 3, 4, 5, 6
COL_GR, COL_GD = 7, 9

TOKEN_TILE = 512
RET_CHUNK_PROMPT = 256
MLP_FF_TILE = 1024

F32 = jnp.float32
BF16 = jnp.bfloat16

_LOG_G = [math.log1p(-(2.0 ** (-5.0 - h))) for h in range(RET_HEADS)]
_SLOPES = np.asarray([2.0 ** (-8.0 * (h + 1) / DA_HEADS) for h in range(DA_HEADS)], np.float32)
_SLOPES_LOG2 = (_SLOPES * np.float32(LOG2E)).astype(np.float32)
_Q_SCALE_LOG2 = (DA_HEAD_DIM ** -0.5) * LOG2E
_NEG = -1e30


def _cparams(n_grid):
    return pltpu.CompilerParams(
        dimension_semantics=("arbitrary",) * n_grid,
        vmem_limit_bytes=V7X_VMEM_LIMIT_BYTES)


def _pick_tile(n, cap):
    t = cap
    while n % t:
        t //= 2
    return t


def _inproj_kernel(x_ref, g_ref, w_ref, cos_ref, sin_ref, qg_ref, kg_ref, bd_ref,
                   p_ref, nk_ref, nv_ref, *rest, emit_t, n_steps):
    if emit_t:
        qt_ref, vt_ref, h_scr, s_scr = rest
    else:
        h_scr, s_scr = rest
    t = pl.program_id(0)
    jb = (jnp.maximum(t, 1) - 1) % N_COL_TILES

    def norm_x():
        x = x_ref[...]
        ms = jnp.mean(x * x, axis=-1, keepdims=True)
        h_scr[...] = (x * lax.rsqrt(ms + NORM_EPS) * g_ref[...]).astype(BF16)

    def matmul():
        s_scr[...] = jnp.dot(h_scr[...], w_ref[...], preferred_element_type=F32)

    def rope(scale):
        cos = cos_ref[...]
        sin = sin_ref[...]
        for hh in range(RET_HEADS):
            sl = slice(hh * RET_DK, (hh + 1) * RET_DK)
            a = s_scr[:, sl]
            o = a * cos + pltpu.roll(a, RET_DK // 2, axis=1) * sin
            if scale != 1.0:
                o = o * scale
            p_ref[:, sl] = o.astype(BF16)

    def plain():
        p_ref[...] = s_scr[...].astype(BF16)

    def head_norm(acc, gain_row):
        sq = (acc * acc).astype(BF16)
        parts = [
            jnp.dot(sq[:, c * V7X_MXU_DIM:(c + 1) * V7X_MXU_DIM], bd_ref[...],
                    preferred_element_type=F32)
            for c in range(COL_W // V7X_MXU_DIM)
        ]
        ms = jnp.concatenate(parts, axis=1)
        return acc * lax.rsqrt(ms + NORM_EPS) * gain_row

    def q_diff():
        n = head_norm(s_scr[...], qg_ref[...])
        p_ref[...] = n.astype(BF16)
        if emit_t:
            qt_ref[0] = (n * _Q_SCALE_LOG2).T.astype(BF16)

    def k_diff():
        n = head_norm(s_scr[...], kg_ref[...])
        if emit_t:
            nk_ref[0] = n.T
        else:
            nk_ref[...] = n
        p_ref[...] = n.astype(BF16)

    def v_diff():
        acc = s_scr[...]
        for h in range(DA_HEADS):
            nv_ref[pl.ds(h, acc.shape[0], stride=DA_HEADS), :] = acc[:, h * RET_DV:(h + 1) * RET_DV]
        p_ref[...] = acc.astype(BF16)
        if emit_t:
            vt_ref[0] = acc.T.astype(BF16)

    epilogues = {COL_QR: lambda: rope(1.0), COL_KR: lambda: rope(RET_DK ** -0.5),
                 COL_QD: q_diff, COL_KD: k_diff, COL_VD: v_diff}
    last_col = N_COL_TILES - 1
    mid = (t > 0) & (t < n_steps - 1)

    @pl.when(t == 0)
    def _():
        norm_x()
        matmul()

    for col, epilogue in epilogues.items():
        @pl.when(mid & (jb == col))
        def _(epilogue=epilogue):
            epilogue()
            matmul()

    @pl.when(mid & ((jb == COL_VR) | (jb == COL_ZR) | ((jb >= COL_GR) & (jb < last_col))))
    def _():
        plain()
        matmul()

    @pl.when(mid & (jb == last_col))
    def _():
        plain()
        norm_x()
        matmul()

    @pl.when(t == n_steps - 1)
    def _():
        plain()


def _inproj(x2d, g_mix, w_bf, cos_tab, sin_tab, qg16, kg16, bd, tm, seq_len):
    emit_t = seq_len is not None
    T = x2d.shape[0]
    n_rope = cos_tab.shape[0] // tm
    n_mm = (T // tm) * N_COL_TILES
    n_steps = n_mm + 1

    def mm_tile(t):
        tt = jnp.minimum(t, n_mm - 1)
        return tt // N_COL_TILES, tt % N_COL_TILES

    def ep_tile(t):
        tt = jnp.maximum(t, 1) - 1
        return tt // N_COL_TILES, tt % N_COL_TILES

    in_specs = [
        pl.BlockSpec((tm, D_MODEL), lambda t: (mm_tile(t)[0], 0)),
        pl.BlockSpec((1, D_MODEL), lambda t: (0, 0)),
        pl.BlockSpec((D_MODEL, COL_W), lambda t: (0, mm_tile(t)[1])),
        pl.BlockSpec((tm, RET_DK), lambda t: (ep_tile(t)[0] % n_rope, 0)),
        pl.BlockSpec((tm, RET_DK), lambda t: (ep_tile(t)[0] % n_rope, 0)),
        pl.BlockSpec((1, COL_W), lambda t: (0, 0)),
        pl.BlockSpec((1, COL_W), lambda t: (0, 0)),
        pl.BlockSpec((V7X_MXU_DIM, V7X_MXU_DIM), lambda t: (0, 0)),
    ]
    if emit_t:
        spb = seq_len // tm
        nk_shape = jax.ShapeDtypeStruct((T // seq_len, COL_W, seq_len), F32)
        nk_spec = pl.BlockSpec((1, COL_W, tm),
                               lambda t: (ep_tile(t)[0] // spb, 0, ep_tile(t)[0] % spb))
    else:
        nk_shape = jax.ShapeDtypeStruct((T, COL_W), F32)
        nk_spec = pl.BlockSpec((tm, COL_W), lambda t: (ep_tile(t)[0], 0))
    out_shape = [
        jax.ShapeDtypeStruct((T, N_COL_TILES * COL_W), BF16),
        nk_shape,
        jax.ShapeDtypeStruct((T * DA_HEADS, RET_DV), F32),
    ]
    out_specs = [
        pl.BlockSpec((tm, COL_W), lambda t: ep_tile(t)),
        nk_spec,
        pl.BlockSpec((tm * DA_HEADS, RET_DV), lambda t: (ep_tile(t)[0], 0)),
    ]
    if emit_t:
        out_shape += [jax.ShapeDtypeStruct((T // tm, COL_W, tm), BF16)] * 2
        out_specs += [pl.BlockSpec((1, COL_W, tm), lambda t: (ep_tile(t)[0], 0, 0))] * 2
    return pl.pallas_call(
        functools.partial(_inproj_kernel, emit_t=emit_t, n_steps=n_steps),
        grid=(n_steps,), in_specs=in_specs, out_specs=out_specs, out_shape=out_shape,
        scratch_shapes=[pltpu.VMEM((tm, D_MODEL), BF16), pltpu.VMEM((tm, COL_W), F32)],
        compiler_params=_cparams(1), name="inproj_t" if emit_t else "inproj",
    )(x2d, g_mix, w_bf, cos_tab, sin_tab, qg16, kg16, bd)


def _ret_kernel(q_ref, k_ref, v_ref, z_ref, gn_ref, *rest, C, has_init):
    if has_init:
        s0_ref, y_ref, s_ref, dm_scr, rd_scr, kd_scr = rest
    else:
        y_ref, s_ref, dm_scr, rd_scr, kd_scr = rest
    b = pl.program_id(0)
    c = pl.program_id(1)

    @pl.when((b == 0) & (c == 0))
    def _():
        li = lax.broadcasted_iota(jnp.int32, (C, C), 0)
        mi = lax.broadcasted_iota(jnp.int32, (C, C), 1)
        causal = li >= mi
        diff = jnp.where(causal, (li - mi).astype(F32), 0.0)
        row = lax.broadcasted_iota(jnp.int32, (C, RET_DK), 0).astype(F32)
        for h in range(RET_HEADS):
            dm_scr[h] = jnp.where(causal, jnp.exp(diff * _LOG_G[h]), 0.0)
            rd_scr[h] = jnp.exp((row + 1.0) * _LOG_G[h])
            kd_scr[h] = jnp.exp((C - 1.0 - row) * _LOG_G[h])

    @pl.when(c == 0)
    def _():
        if has_init:
            s_ref[...] = s0_ref[...]
        else:
            s_ref[...] = jnp.zeros(s_ref.shape, F32)

    for h in range(RET_HEADS):
        sl = slice(h * RET_DK, (h + 1) * RET_DK)
        q = q_ref[:, sl]
        k = k_ref[:, sl]
        v = v_ref[:, sl]
        state = s_ref[0, h]
        sc = lax.dot_general(q, k, (((1,), (1,)), ((), ())), preferred_element_type=F32)
        sc = sc * dm_scr[h]
        intra = jnp.dot(sc.astype(BF16), v, preferred_element_type=F32)
        inter = jnp.dot(q, state.astype(BF16), preferred_element_type=F32) * rd_scr[h]
        o = intra + inter
        kw = (k.astype(F32) * kd_scr[h]).astype(BF16)
        upd = lax.dot_general(kw, v, (((0,), (0,)), ((), ())), preferred_element_type=F32)
        s_ref[0, h] = state * math.exp(C * _LOG_G[h]) + upd
        ms = jnp.mean(o * o, axis=-1, keepdims=True)
        z = z_ref[:, sl].astype(F32)
        y = o * lax.rsqrt(ms + NORM_EPS) * gn_ref[:, sl] * (z * jax.nn.sigmoid(z))
        y_ref[:, sl] = y.astype(BF16)


def _retention(p, ret_gn_row, n_seq, seq_len, C, init_state):
    n_chunks = seq_len // C
    has_init = init_state is not None

    def col(t):
        return pl.BlockSpec((C, COL_W), lambda b, c: (b * n_chunks + c, t))

    in_specs = [col(COL_QR), col(COL_KR), col(COL_VR), col(COL_ZR),
                pl.BlockSpec((1, COL_W), lambda b, c: (0, 0))]
    args = [p, p, p, p, ret_gn_row]
    state_spec = pl.BlockSpec((1, RET_HEADS, RET_DK, RET_DV), lambda b, c: (b, 0, 0, 0))
    if has_init:
        in_specs.append(state_spec)
        args.append(init_state)
    return pl.pallas_call(
        functools.partial(_ret_kernel, C=C, has_init=has_init),
        grid=(n_seq, n_chunks), in_specs=in_specs,
        out_specs=[pl.BlockSpec((C, COL_W), lambda b, c: (b * n_chunks + c, 0)), state_spec],
        out_shape=[jax.ShapeDtypeStruct((n_seq * seq_len, COL_W), BF16),
                   jax.ShapeDtypeStruct((n_seq, RET_HEADS, RET_DK, RET_DV), F32)],
        scratch_shapes=[pltpu.VMEM((RET_HEADS, C, C), F32),
                        pltpu.VMEM((RET_HEADS, C, RET_DK), F32),
                        pltpu.VMEM((RET_HEADS, C, RET_DK), F32)],
        compiler_params=_cparams(2), name="retention_init" if has_init else "retention",
    )(*args)


def _lambda_value(lq1, lk1, lq2, lk2, lam_init):
    a = jnp.sum(lq1[...] * lk1[...], axis=-1, keepdims=True)
    b = jnp.sum(lq2[...] * lk2[...], axis=-1, keepdims=True)
    return jnp.exp(a) - jnp.exp(b) + lam_init


def _bf16_split3(x):
    out, r = [], np.float64(x)
    for _ in range(3):
        c = np.float64(np.asarray(r, np.float32).astype(jnp.bfloat16).astype(np.float32))
        out.append(float(c))
        r = r - c
    return out


def _alibi_operands(blk):
    c = _bf16_split3(LOG2E)
    pos = np.arange(blk)
    ek = np.zeros((blk, 2 * DA_HEAD_DIM), np.float32)
    bq = np.zeros((2 * DA_HEAD_DIM, blk), np.float32)
    for t in range(3):
        ek[:, t] = pos >> 4
        ek[:, 3 + t] = pos & 15
        ek[:, 6 + t] = -16.0 * c[t]
        ek[:, 9 + t] = -c[t]
        bq[t, :] = 16.0 * c[t]
        bq[3 + t, :] = c[t]
        bq[6 + t, :] = pos >> 4
        bq[9 + t, :] = pos & 15
    return jnp.asarray(ek, BF16), jnp.asarray(bq, BF16)


def _attn_kernel(slog2_ref, spow2_ref, qt1_ref, qt2_ref, k1_ref, k2_ref, vt_ref, ek_ref, bq_ref,
                 lq1, lk1, lq2, lk2, gn_ref, o_ref,
                 qx_scr, m_scr, l_scr, acc_scr, s_scr, mb_scr, *, blk, lam_init):
    j = pl.program_id(1)
    qi = pl.program_id(2)
    half = DA_HEAD_DIM
    hd2 = 2 * DA_HEAD_DIM
    k_refs = (k1_ref, k2_ref)

    zeros = jnp.zeros((half, blk), BF16)
    for hh in range(2):
        bias_rows = (bq_ref[...].astype(F32) * spow2_ref[2 * j + hh]).astype(BF16)
        for c, qref in enumerate((qt1_ref, qt2_ref)):
            idx = 2 * c + hh
            if hh == 0:
                qx_scr[idx, :half, :] = qref[0, :half, :]
                qx_scr[idx, half:hd2, :] = zeros
            else:
                qx_scr[idx, :half, :] = zeros
                qx_scr[idx, half:hd2, :] = qref[0, half:, :]
            qx_scr[idx, hd2:, :] = bias_rows
    m_scr[...] = jnp.full(m_scr.shape, _NEG, F32)
    l_scr[...] = jnp.zeros(l_scr.shape, F32)
    acc_scr[...] = jnp.zeros(acc_scr.shape, F32)

    def score_diag(par):
        kk = lax.broadcasted_iota(jnp.int32, (blk, blk), 0)
        qq = lax.broadcasted_iota(jnp.int32, (blk, blk), 1)
        allowed = (kk // CHUNK) <= (qq // CHUNK)
        dist = jnp.abs(kk - qq).astype(F32)
        row0 = pl.multiple_of(qi * blk, blk)
        for c in range(2):
            kc = k_refs[c][pl.ds(row0, blk), :]
            for hh in range(2):
                idx = 2 * c + hh
                s = jnp.dot(kc, qx_scr[idx, :hd2, :], preferred_element_type=F32)
                s = jnp.where(allowed, s - slog2_ref[2 * j + hh] * dist, _NEG)
                s_scr[par, idx] = s
                mb_scr[par, idx] = jnp.max(s, axis=0, keepdims=True)

    def score_off(ki, par):
        row0 = pl.multiple_of(ki * blk, blk)
        for c in range(2):
            kx = jnp.concatenate([k_refs[c][pl.ds(row0, blk), :], ek_ref[...]], axis=1)
            for hh in range(2):
                idx = 2 * c + hh
                s = jnp.dot(kx, qx_scr[idx], preferred_element_type=F32)
                s_scr[par, idx] = s
                mb_scr[par, idx] = jnp.max(s, axis=0, keepdims=True)

    def consume(kb, par):
        dblk = ((qi - kb) * blk).astype(F32)
        for c in range(2):
            for hh in range(2):
                idx = 2 * c + hh
                shift = slog2_ref[2 * j + hh] * dblk
                m_old = m_scr[idx]
                m_new = jnp.maximum(m_old, mb_scr[par, idx] - shift)
                alpha = jnp.exp2(m_old - m_new)
                p = jnp.exp2(s_scr[par, idx] - (m_new + shift))
                l_scr[idx] = alpha * l_scr[idx] + jnp.sum(p, axis=0, keepdims=True)
                vt_h = vt_ref[kb, hh * hd2:(hh + 1) * hd2, :]
                acc_scr[idx] = alpha * acc_scr[idx] + jnp.dot(
                    vt_h, p.astype(BF16), preferred_element_type=F32)
                m_scr[idx] = m_new

    score_diag(0)

    def pair_body(u, carry):
        t0 = 2 * u
        score_off(t0, 1)
        consume(jnp.where(t0 == 0, qi, t0 - 1), 0)
        score_off(t0 + 1, 0)
        consume(t0, 1)
        return carry

    lax.fori_loop(0, qi // 2, pair_body, 0)

    @pl.when(qi % 2 == 1)
    def _():
        t0 = qi - 1
        score_off(t0, 1)
        consume(jnp.where(t0 == 0, qi, t0 - 1), 0)
        consume(t0, 1)

    @pl.when(qi % 2 == 0)
    def _():
        consume(jnp.where(qi == 0, qi, qi - 1), 0)

    lam = _lambda_value(lq1, lk1, lq2, lk2, lam_init)
    for hh in range(2):
        o = acc_scr[hh] / l_scr[hh] - lam * (acc_scr[2 + hh] / l_scr[2 + hh])
        ms = jnp.mean(o * o, axis=0, keepdims=True)
        on = (o * lax.rsqrt(ms + NORM_EPS)).T
        sl = slice(hh * hd2, (hh + 1) * hd2)
        o_ref[:, sl] = (on * gn_ref[:, sl] * (1.0 - lam_init)).astype(BF16)


def _attention_prompt(p, qt, vt, lam_rows, da_gn_row, n_seq, seq_len, blk, lam_init):
    nb = seq_len // blk
    n_hp = DA_HEADS // 2
    hd2 = 2 * DA_HEAD_DIM
    kcol0 = COL_KD * COL_W // hd2
    ek, bq = _alibi_operands(blk)

    smem = pl.BlockSpec(memory_space=pltpu.SMEM)
    in_specs = [
        smem, smem,
        pl.BlockSpec((1, hd2, blk), lambda b, j, q: (b * nb + q, j, 0)),
        pl.BlockSpec((1, hd2, blk), lambda b, j, q: (b * nb + q, n_hp + j, 0)),
        pl.BlockSpec((seq_len, hd2), lambda b, j, q: (b, kcol0 + j)),
        pl.BlockSpec((seq_len, hd2), lambda b, j, q: (b, kcol0 + n_hp + j)),
        pl.BlockSpec((nb, 2 * hd2, blk), lambda b, j, q: (b, j, 0)),
        pl.BlockSpec((blk, hd2), lambda b, j, q: (0, 0)),
        pl.BlockSpec((hd2, blk), lambda b, j, q: (0, 0)),
    ] + [pl.BlockSpec((1, DA_HEAD_DIM), lambda b, j, q: (0, 0))] * 4 + [
        pl.BlockSpec((1, 2 * hd2), lambda b, j, q: (0, j)),
    ]
    return pl.pallas_call(
        functools.partial(_attn_kernel, blk=blk, lam_init=lam_init),
        grid=(n_seq, n_hp, nb), in_specs=in_specs,
        out_specs=pl.BlockSpec((blk, 2 * hd2), lambda b, j, q: (b * nb + q, j)),
        out_shape=jax.ShapeDtypeStruct((n_seq * seq_len, COL_W), BF16),
        scratch_shapes=[pltpu.VMEM((4, 2 * hd2, blk), BF16),
                        pltpu.VMEM((4, 1, blk), F32),
                        pltpu.VMEM((4, 1, blk), F32),
                        pltpu.VMEM((4, hd2, blk), F32),
                        pltpu.VMEM((2, 4, blk, blk), F32),
                        pltpu.VMEM((2, 4, 1, blk), F32)],
        compiler_params=_cparams(3), name="diff_attn_prompt",
    )(jnp.asarray(_SLOPES_LOG2), jnp.asarray(_SLOPES), qt, qt, p, p, vt, ek, bq,
      *lam_rows, da_gn_row)


def _sattn_kernel(slopes_ref, q_ref, kn_ref, vn_ref, kc_ref, vc_ref,
                  lq1, lk1, lq2, lk2, gn_ref, o_ref, *, L, P, lam_init):
    lam = _lambda_value(lq1, lk1, lq2, lk2, lam_init)
    hd2 = 2 * DA_HEAD_DIM
    lane = lax.broadcasted_iota(jnp.int32, (L, hd2), 1)
    qpos_c = P + lax.broadcasted_iota(jnp.int32, (L, P), 0)
    kpos_c = lax.broadcasted_iota(jnp.int32, (L, P), 1)
    qpos_n = P + lax.broadcasted_iota(jnp.int32, (L, L), 0)
    kpos_n = P + lax.broadcasted_iota(jnp.int32, (L, L), 1)
    dist_c = jnp.abs(qpos_c - kpos_c).astype(F32)
    dist_n = jnp.abs(qpos_n - kpos_n).astype(F32)
    ok_c = (kpos_c // CHUNK) <= (qpos_c // CHUNK)
    ok_n = (kpos_n // CHUNK) <= (qpos_n // CHUNK)
    nt = (((1,), (1,)), ((), ()))
    for pr in range(DA_HEADS // 2):
        for hh in range(2):
            h = 2 * pr + hh
            slope = slopes_ref[h]
            probs = []
            for c in range(2):
                sl = slice((c * (DA_HEADS // 2) + pr) * hd2, (c * (DA_HEADS // 2) + pr + 1) * hd2)
                q128 = q_ref[:, sl].astype(F32) * _Q_SCALE_LOG2
                keep = (lane < DA_HEAD_DIM) if hh == 0 else (lane >= DA_HEAD_DIM)
                qm = jnp.where(keep, q128, 0.0).astype(BF16)
                kc_t = kc_ref[0, sl, :].astype(BF16)
                kn = kn_ref[:, sl]
                s_c = jnp.dot(qm, kc_t, preferred_element_type=F32)
                s_n = lax.dot_general(qm, kn, nt, preferred_element_type=F32)
                s_c = jnp.where(ok_c, s_c - slope * dist_c, _NEG)
                s_n = jnp.where(ok_n, s_n - slope * dist_n, _NEG)
                m = jnp.maximum(jnp.max(s_c, axis=-1, keepdims=True),
                                jnp.max(s_n, axis=-1, keepdims=True))
                p_c = jnp.exp2(s_c - m)
                p_n = jnp.exp2(s_n - m)
                inv = 1.0 / (jnp.sum(p_c, axis=-1, keepdims=True)
                             + jnp.sum(p_n, axis=-1, keepdims=True))
                probs.append((p_c * inv, p_n * inv))
            a_c = (probs[0][0] - lam * probs[1][0]).astype(BF16)
            a_n = (probs[0][1] - lam * probs[1][1]).astype(BF16)
            vsl = slice(h * hd2, (h + 1) * hd2)
            vc = vc_ref[0, pl.ds(h, P, stride=DA_HEADS), :].astype(BF16)
            o = (jnp.dot(a_c, vc, preferred_element_type=F32)
                 + jnp.dot(a_n, vn_ref[:, vsl], preferred_element_type=F32))
            ms = jnp.mean(o * o, axis=-1, keepdims=True)
            on = o * lax.rsqrt(ms + NORM_EPS) * gn_ref[:, vsl] * (1.0 - lam_init)
            o_ref[:, vsl] = on.astype(BF16)


def _attention_sample(p, cache_kt, cache_v, lam_rows, da_gn_row, n_seq, L, lam_init):
    P = cache_kt.shape[2]
    smem = pl.BlockSpec(memory_space=pltpu.SMEM)
    in_specs = [
        smem,
        pl.BlockSpec((L, COL_W), lambda s: (s, COL_QD)),
        pl.BlockSpec((L, COL_W), lambda s: (s, COL_KD)),
        pl.BlockSpec((L, COL_W), lambda s: (s, COL_VD)),
        pl.BlockSpec((1, COL_W, P), lambda s: (s, 0, 0)),
        pl.BlockSpec((1, P * DA_HEADS, RET_DV), lambda s: (s, 0, 0)),
    ] + [pl.BlockSpec((1, DA_HEAD_DIM), lambda s: (0, 0))] * 4 + [
        pl.BlockSpec((1, COL_W), lambda s: (0, 0)),
    ]
    return pl.pallas_call(
        functools.partial(_sattn_kernel, L=L, P=P, lam_init=lam_init),
        grid=(n_seq,), in_specs=in_specs,
        out_specs=pl.BlockSpec((L, COL_W), lambda s: (s, 0)),
        out_shape=jax.ShapeDtypeStruct((n_seq * L, COL_W), BF16),
        compiler_params=_cparams(1), name="diff_attn_sample",
    )(jnp.asarray(_SLOPES_LOG2), p, p, p, cache_kt, cache_v, *lam_rows, da_gn_row)


def _merge_kernel(x_ref, yr_ref, yd_ref, gr0, gr1, gd0, gd1, wr_ref, wd_ref, wo_ref, gm_ref,
                  x1_ref, h2_ref):
    yr = jnp.dot(yr_ref[...], wr_ref[...], preferred_element_type=F32)
    yd = jnp.dot(yd_ref[...], wd_ref[...], preferred_element_type=F32)
    gr = jnp.concatenate([gr0[...], gr1[...]], axis=1).astype(F32)
    gd = jnp.concatenate([gd0[...], gd1[...]], axis=1).astype(F32)
    m = jax.nn.sigmoid(gr) * yr + jax.nn.sigmoid(gd) * yd
    x1 = x_ref[...] + jnp.dot(m.astype(BF16), wo_ref[...], preferred_element_type=F32)
    x1_ref[...] = x1
    ms = jnp.mean(x1 * x1, axis=-1, keepdims=True)
    h2_ref[...] = (x1 * lax.rsqrt(ms + NORM_EPS) * gm_ref[...]).astype(BF16)


def _merge(x2d, yr_in, yd_in, p, w_ret_o, w_da_o, w_out, g_mlp_row, tm):
    T = x2d.shape[0]

    def whole(shape):
        return pl.BlockSpec(shape, lambda i: (0, 0), pipeline_mode=pl.Buffered(1))

    def pcol(t):
        return pl.BlockSpec((tm, COL_W), lambda i: (i, t))

    in_specs = [
        pl.BlockSpec((tm, D_MODEL), lambda i: (i, 0)),
        pl.BlockSpec((tm, COL_W), lambda i: (i, 0)),
        pl.BlockSpec((tm, COL_W), lambda i: (i, 0)),
        pcol(COL_GR), pcol(COL_GR + 1), pcol(COL_GD), pcol(COL_GD + 1),
        whole((COL_W, D_MODEL)), whole((COL_W, D_MODEL)), whole((D_MODEL, D_MODEL)),
        pl.BlockSpec((1, D_MODEL), lambda i: (0, 0)),
    ]
    return pl.pallas_call(
        _merge_kernel, grid=(T // tm,), in_specs=in_specs,
        out_specs=[pl.BlockSpec((tm, D_MODEL), lambda i: (i, 0)),
                   pl.BlockSpec((tm, D_MODEL), lambda i: (i, 0))],
        out_shape=[jax.ShapeDtypeStruct((T, D_MODEL), F32),
                   jax.ShapeDtypeStruct((T, D_MODEL), BF16)],
        compiler_params=_cparams(1), name="merge",
    )(x2d, yr_in, yd_in, p, p, p, p, w_ret_o, w_da_o, w_out, g_mlp_row)


def _mlp_kernel(x1_ref, h2_ref, wu_ref, wd_ref, y_ref, r_scr, *, n_steps, nf):
    t = pl.program_id(0)
    fb = (jnp.maximum(t, 1) - 1) % nf

    def up(par):
        u = jnp.dot(h2_ref[...], wu_ref[...], preferred_element_type=F32)
        r_scr[par] = jnp.square(jnp.maximum(u, 0.0)).astype(BF16)

    def down(par, first_chunk):
        contrib = jnp.dot(r_scr[par], wd_ref[...], preferred_element_type=F32)
        if first_chunk:
            y_ref[...] = x1_ref[...] + contrib
        else:
            y_ref[...] += contrib

    mid = (t > 0) & (t < n_steps - 1)

    @pl.when(t == 0)
    def _():
        up(0)

    @pl.when(mid & (t % 2 == 0))
    def _():
        down(1, False)
        up(0)

    @pl.when(mid & (t % 2 == 1) & (fb == 0))
    def _():
        down(0, True)
        up(1)

    @pl.when(mid & (t % 2 == 1) & (fb != 0))
    def _():
        down(0, False)
        up(1)

    @pl.when(t == n_steps - 1)
    def _():
        down((n_steps - 2) % 2, False)


def _mlp(x1, h2, w_up, w_down, tm, tf):
    T = x1.shape[0]
    nf = D_FF // tf
    assert nf % 2 == 0 and nf >= 2
    n_up = (T // tm) * nf
    n_steps = n_up + 1

    def up_idx(t):
        return jnp.minimum(t, n_up - 1)

    def down_idx(t):
        return jnp.maximum(t, 1) - 1

    return pl.pallas_call(
        functools.partial(_mlp_kernel, n_steps=n_steps, nf=nf), grid=(n_steps,),
        in_specs=[pl.BlockSpec((tm, D_MODEL), lambda t: (down_idx(t) // nf, 0)),
                  pl.BlockSpec((tm, D_MODEL), lambda t: (up_idx(t) // nf, 0)),
                  pl.BlockSpec((D_MODEL, tf), lambda t: (0, up_idx(t) % nf)),
                  pl.BlockSpec((tf, D_MODEL), lambda t: (down_idx(t) % nf, 0))],
        out_specs=pl.BlockSpec((tm, D_MODEL), lambda t: (down_idx(t) // nf, 0)),
        out_shape=jax.ShapeDtypeStruct((T, D_MODEL), F32),
        scratch_shapes=[pltpu.VMEM((2, tm, tf), BF16)],
        compiler_params=_cparams(1), name="mlp",
    )(x1, h2, w_up, w_down)


def _rope_tables(pos):
    half = RET_DK // 2
    inv = jnp.exp(-math.log(ROPE_BASE) * jnp.arange(half, dtype=F32) / half)
    ang = pos[:, None] * inv[None, :]
    cos = jnp.cos(ang)
    sin = jnp.sin(ang)
    return jnp.concatenate([cos, cos], axis=1), jnp.concatenate([-sin, sin], axis=1)


def _block_diag_mean():
    r = np.arange(V7X_MXU_DIM) // DA_HEAD_DIM
    return jnp.asarray((r[:, None] == r[None, :]).astype(np.float32) / DA_HEAD_DIM, BF16)


def kernel(x_prompt, x_sample, cache_da_k, cache_da_v, state_ret, g_mix, w_in, ret_gn, qn_g, kn_g,
           lambda_q1, lambda_k1, lambda_q2, lambda_k2, da_gn, w_ret_o, w_da_o, w_out, g_mlp,
           w_up, w_down):
    B, S, _ = x_prompt.shape
    DB, L, _ = x_sample.shape
    P = cache_da_k.shape[2]
    depth = w_in.shape[0]
    assert S % TOKEN_TILE == 0 and S % RET_CHUNK_PROMPT == 0 and L % 16 == 0

    tm_p = TOKEN_TILE
    tm_s = _pick_tile(DB * L, TOKEN_TILE)
    assert tm_s % L == 0 or L % tm_s == 0
    cos_p, sin_p = _rope_tables(jnp.arange(S, dtype=F32))
    cos_s, sin_s = _rope_tables(P + jnp.arange(L, dtype=F32))
    if tm_s > L:
        cos_s = jnp.tile(cos_s, (tm_s // L, 1))
        sin_s = jnp.tile(sin_s, (tm_s // L, 1))
    bd = _block_diag_mean()
    n_heads_qk = COL_W // DA_HEAD_DIM

    xp = x_prompt.reshape(B * S, D_MODEL)
    xs = x_sample.reshape(DB * L, D_MODEL)
    kp_l, vp_l, rp_l, ks_l, vs_l, rs_l = [], [], [], [], [], []
    for l in range(depth):
        lam_init = 0.8 - 0.6 * math.exp(-0.3 * l)
        w_in_bf = w_in[l].astype(BF16)
        w_ro, w_do, w_o = w_ret_o[l].astype(BF16), w_da_o[l].astype(BF16), w_out[l].astype(BF16)
        w_u, w_d = w_up[l].astype(BF16), w_down[l].astype(BF16)
        g_mix_row = g_mix[l].reshape(1, D_MODEL)
        g_mlp_row = g_mlp[l].reshape(1, D_MODEL)
        qg16 = jnp.tile(qn_g[l], n_heads_qk).reshape(1, COL_W)
        kg16 = jnp.tile(kn_g[l], n_heads_qk).reshape(1, COL_W)
        ret_gn_row = ret_gn[l].reshape(1, COL_W)
        da_gn_row = da_gn[l].reshape(1, COL_W)
        lam_rows = [a[l].astype(F32).reshape(1, DA_HEAD_DIM)
                    for a in (lambda_q1, lambda_k1, lambda_q2, lambda_k2)]

        p, nkt, nv, qt, vt = _inproj(xp, g_mix_row, w_in_bf, cos_p, sin_p, qg16, kg16, bd,
                                     tm_p, S)
        yr_in, s_fin = _retention(p, ret_gn_row, B, S, RET_CHUNK_PROMPT, None)
        yd_in = _attention_prompt(p, qt, vt, lam_rows, da_gn_row, B, S, TOKEN_TILE, lam_init)
        x1, h2 = _merge(xp, yr_in, yd_in, p, w_ro, w_do, w_o, g_mlp_row, tm_p)
        xp = _mlp(x1, h2, w_u, w_d, tm_p, MLP_FF_TILE)
        nk = jnp.transpose(nkt.reshape(B, 2 * DA_HEADS, DA_HEAD_DIM, S), (0, 3, 1, 2))
        kp_l.append(nk)
        vp_l.append(nv.reshape(B, S, DA_HEADS, 2 * DA_HEAD_DIM))
        rp_l.append(s_fin)

        p, nk, nv = _inproj(xs, g_mix_row, w_in_bf, cos_s, sin_s, qg16, kg16, bd, tm_s, None)
        yr_in, s_new = _retention(p, ret_gn_row, DB, L, L, state_ret[l].astype(F32))
        cache_kt = jnp.transpose(cache_da_k[l].reshape(DB, P, COL_W), (0, 2, 1))
        yd_in = _attention_sample(p, cache_kt, cache_da_v[l].reshape(DB, P * DA_HEADS, RET_DV),
                                  lam_rows, da_gn_row, DB, L, lam_init)
        x1, h2 = _merge(xs, yr_in, yd_in, p, w_ro, w_do, w_o, g_mlp_row, tm_s)
        xs = _mlp(x1, h2, w_u, w_d, tm_s, MLP_FF_TILE)
        ks_l.append(nk.reshape(DB, L, 2 * DA_HEADS, DA_HEAD_DIM))
        vs_l.append(nv.reshape(DB, L, DA_HEADS, 2 * DA_HEAD_DIM))
        rs_l.append(s_new)

    return (xp.reshape(B, S, D_MODEL), xs.reshape(DB, L, D_MODEL),
            jnp.stack(kp_l, 0), jnp.stack(vp_l, 0), jnp.stack(rp_l, 0),
            jnp.stack(ks_l, 0), jnp.stack(vs_l, 0), jnp.stack(rs_l, 0))
```

```python
import functools
import math

import numpy as np
import jax
import jax.numpy as jnp
from jax import lax
from jax.experimental import pallas as pl
from jax.experimental.pallas import tpu as pltpu

D_MODEL = 2048
CHUNK = 64
RET_HEADS = 8
RET_DK = 128
RET_DV = 128
DA_HEADS = 8
DA_HEAD_DIM = 64
D_FF = 4 * D_MODEL
ROPE_BASE = 10000.0
NORM_EPS = 1e-6
LOG2E = 1.4426950408889634

V7X_LANES = 128
V7X_MXU_DIM = 256
V7X_VMEM_LIMIT_BYTES = 56 * 1024 * 1024

COL_W = 1024
N_COL_TILES = 11
COL_QR, COL_KR, COL_VR, COL_ZR, COL_QD, COL_KD, COL_VD = 0, 1, 2, 3, 4, 5, 6
COL_GR, COL_GD = 7, 9

TOKEN_TILE = 512
RET_CHUNK_PROMPT = 256
MLP_FF_TILE = 1024

F32 = jnp.float32
BF16 = jnp.bfloat16

_LOG_G = [math.log1p(-(2.0 ** (-5.0 - h))) for h in range(RET_HEADS)]
_SLOPES = np.asarray([2.0 ** (-8.0 * (h + 1) / DA_HEADS) for h in range(DA_HEADS)], np.float32)
_SLOPES_LOG2 = (_SLOPES * np.float32(LOG2E)).astype(np.float32)
_Q_SCALE_LOG2 = (DA_HEAD_DIM ** -0.5) * LOG2E
_NEG = -1e30


def _cparams(n_grid):
    return pltpu.CompilerParams(
        dimension_semantics=("arbitrary",) * n_grid,
        vmem_limit_bytes=V7X_VMEM_LIMIT_BYTES)


def _pick_tile(n, cap):
    t = cap
    while n % t:
        t //= 2
    return t


def _rmsnorm_kernel(x_ref, g_ref, h_ref):
    x = x_ref[...]
    ms = jnp.mean(x * x, axis=-1, keepdims=True)
    h_ref[...] = (x * lax.rsqrt(ms + NORM_EPS) * g_ref[...]).astype(BF16)


def _rmsnorm(x2d, g_row, tm):
    T = x2d.shape[0]
    return pl.pallas_call(
        _rmsnorm_kernel, grid=(T // tm,),
        in_specs=[pl.BlockSpec((tm, D_MODEL), lambda i: (i, 0)),
                  pl.BlockSpec((1, D_MODEL), lambda i: (0, 0))],
        out_specs=pl.BlockSpec((tm, D_MODEL), lambda i: (i, 0)),
        out_shape=jax.ShapeDtypeStruct((T, D_MODEL), BF16),
        compiler_params=_cparams(1), name="rmsnorm",
    )(x2d, g_row)


def _inproj_kernel(h_ref, w_ref, cos_ref, sin_ref, qg_ref, kg_ref, bd_ref,
                   p_ref, nk_ref, nv_ref, *rest, emit_t, n_steps, n_tok):
    if emit_t:
        qt_ref, vt_ref, s_scr = rest
    else:
        (s_scr,) = rest
    t = pl.program_id(0)
    jb = (jnp.maximum(t, 1) - 1) // n_tok

    def matmul():
        s_scr[...] = jnp.dot(h_ref[...], w_ref[...], preferred_element_type=F32)

    def rope(scale):
        cos = cos_ref[...]
        sin = sin_ref[...]
        for hh in range(RET_HEADS):
            sl = slice(hh * RET_DK, (hh + 1) * RET_DK)
            a = s_scr[:, sl]
            o = a * cos + pltpu.roll(a, RET_DK // 2, axis=1) * sin
            if scale != 1.0:
                o = o * scale
            p_ref[:, sl] = o.astype(BF16)

    def plain():
        p_ref[...] = s_scr[...].astype(BF16)

    def head_norm(acc, gain_row):
        sq = (acc * acc).astype(BF16)
        parts = [
            jnp.dot(sq[:, c * V7X_MXU_DIM:(c + 1) * V7X_MXU_DIM], bd_ref[...],
                    preferred_element_type=F32)
            for c in range(COL_W // V7X_MXU_DIM)
        ]
        ms = jnp.concatenate(parts, axis=1)
        return acc * lax.rsqrt(ms + NORM_EPS) * gain_row

    def q_diff():
        n = head_norm(s_scr[...], qg_ref[...])
        p_ref[...] = n.astype(BF16)
        if emit_t:
            qt_ref[0] = (n * _Q_SCALE_LOG2).T.astype(BF16)

    def k_diff():
        n = head_norm(s_scr[...], kg_ref[...])
        if emit_t:
            nk_ref[0] = n.T
        else:
            nk_ref[...] = n
        p_ref[...] = n.astype(BF16)

    def v_diff():
        acc = s_scr[...]
        for h in range(DA_HEADS):
            nv_ref[pl.ds(h, acc.shape[0], stride=DA_HEADS), :] = acc[:, h * RET_DV:(h + 1) * RET_DV]
        p_ref[...] = acc.astype(BF16)
        if emit_t:
            vt_ref[0] = acc.T.astype(BF16)

    epilogues = {COL_QR: lambda: rope(1.0), COL_KR: lambda: rope(RET_DK ** -0.5),
                 COL_QD: q_diff, COL_KD: k_diff, COL_VD: v_diff}
    mid = (t > 0) & (t < n_steps - 1)

    @pl.when(t == 0)
    def _():
        matmul()

    for col, epilogue in epilogues.items():
        @pl.when(mid & (jb == col))
        def _(epilogue=epilogue):
            epilogue()
            matmul()

    @pl.when(mid & ((jb == COL_VR) | (jb == COL_ZR) | (jb >= COL_GR)))
    def _():
        plain()
        matmul()

    @pl.when(t == n_steps - 1)
    def _():
        plain()


def _inproj(h2d, w_bf, cos_tab, sin_tab, qg16, kg16, bd, tm, seq_len):
    emit_t = seq_len is not None
    T = h2d.shape[0]
    n_tok = T // tm
    n_rope = cos_tab.shape[0] // tm
    n_mm = n_tok * N_COL_TILES
    n_steps = n_mm + 1

    def mm_tile(t):
        tt = jnp.minimum(t, n_mm - 1)
        return tt // n_tok, tt % n_tok

    def ep_tile(t):
        tt = jnp.maximum(t, 1) - 1
        return tt // n_tok, tt % n_tok

    def side_tile(t, col):
        j, i = ep_tile(t)
        return jnp.where(j < col, 0, jnp.where(j > col, n_tok - 1, i))

    in_specs = [
        pl.BlockSpec((tm, D_MODEL), lambda t: (mm_tile(t)[1], 0)),
        pl.BlockSpec((D_MODEL, COL_W), lambda t: (0, mm_tile(t)[0])),
        pl.BlockSpec((tm, RET_DK), lambda t: (ep_tile(t)[1] % n_rope, 0)),
        pl.BlockSpec((tm, RET_DK), lambda t: (ep_tile(t)[1] % n_rope, 0)),
        pl.BlockSpec((1, COL_W), lambda t: (0, 0)),
        pl.BlockSpec((1, COL_W), lambda t: (0, 0)),
        pl.BlockSpec((V7X_MXU_DIM, V7X_MXU_DIM), lambda t: (0, 0)),
    ]
    if emit_t:
        spb = seq_len // tm
        nk_shape = jax.ShapeDtypeStruct((T // seq_len, COL_W, seq_len), F32)
        nk_spec = pl.BlockSpec((1, COL_W, tm), lambda t: (side_tile(t, COL_KD) // spb, 0,
                                                          side_tile(t, COL_KD) % spb))
    else:
        nk_shape = jax.ShapeDtypeStruct((T, COL_W), F32)
        nk_spec = pl.BlockSpec((tm, COL_W), lambda t: (side_tile(t, COL_KD), 0))
    out_shape = [
        jax.ShapeDtypeStruct((T, N_COL_TILES * COL_W), BF16),
        nk_shape,
        jax.ShapeDtypeStruct((T * DA_HEADS, RET_DV), F32),
    ]
    out_specs = [
        pl.BlockSpec((tm, COL_W), lambda t: (ep_tile(t)[1], ep_tile(t)[0])),
        nk_spec,
        pl.BlockSpec((tm * DA_HEADS, RET_DV), lambda t: (side_tile(t, COL_VD), 0)),
    ]
    if emit_t:
        out_shape += [jax.ShapeDtypeStruct((T // tm, COL_W, tm), BF16)] * 2
        out_specs += [pl.BlockSpec((1, COL_W, tm), lambda t: (side_tile(t, COL_QD), 0, 0)),
                      pl.BlockSpec((1, COL_W, tm), lambda t: (side_tile(t, COL_VD), 0, 0))]
    return pl.pallas_call(
        functools.partial(_inproj_kernel, emit_t=emit_t, n_steps=n_steps, n_tok=n_tok),
        grid=(n_steps,), in_specs=in_specs, out_specs=out_specs, out_shape=out_shape,
        scratch_shapes=[pltpu.VMEM((tm, COL_W), F32)],
        compiler_params=_cparams(1), name="inproj_t" if emit_t else "inproj",
    )(h2d, w_bf, cos_tab, sin_tab, qg16, kg16, bd)


def _ret_kernel(q_ref, k_ref, v_ref, z_ref, gn_ref, *rest, C, has_init):
    if has_init:
        s0_ref, y_ref, s_ref, dm_scr, rd_scr, kd_scr = rest
    else:
        y_ref, s_ref, dm_scr, rd_scr, kd_scr = rest
    b = pl.program_id(0)
    c = pl.program_id(1)

    @pl.when((b == 0) & (c == 0))
    def _():
        li = lax.broadcasted_iota(jnp.int32, (C, C), 0)
        mi = lax.broadcasted_iota(jnp.int32, (C, C), 1)
        causal = li >= mi
        diff = jnp.where(causal, (li - mi).astype(F32), 0.0)
        row = lax.broadcasted_iota(jnp.int32, (C, RET_DK), 0).astype(F32)
        for h in range(RET_HEADS):
            dm_scr[h] = jnp.where(causal, jnp.exp(diff * _LOG_G[h]), 0.0)
            rd_scr[h] = jnp.exp((row + 1.0) * _LOG_G[h])
            kd_scr[h] = jnp.exp((C - 1.0 - row) * _LOG_G[h])

    @pl.when(c == 0)
    def _():
        if has_init:
            s_ref[...] = s0_ref[...]
        else:
            s_ref[...] = jnp.zeros(s_ref.shape, F32)

    for h in range(RET_HEADS):
        sl = slice(h * RET_DK, (h + 1) * RET_DK)
        q = q_ref[:, sl]
        k = k_ref[:, sl]
        v = v_ref[:, sl]
        state = s_ref[0, h]
        sc = lax.dot_general(q, k, (((1,), (1,)), ((), ())), preferred_element_type=F32)
        sc = sc * dm_scr[h]
        intra = jnp.dot(sc.astype(BF16), v, preferred_element_type=F32)
        inter = jnp.dot(q, state.astype(BF16), preferred_element_type=F32) * rd_scr[h]
        o = intra + inter
        kw = (k.astype(F32) * kd_scr[h]).astype(BF16)
        upd = lax.dot_general(kw, v, (((0,), (0,)), ((), ())), preferred_element_type=F32)
        s_ref[0, h] = state * math.exp(C * _LOG_G[h]) + upd
        ms = jnp.mean(o * o, axis=-1, keepdims=True)
        z = z_ref[:, sl].astype(F32)
        y = o * lax.rsqrt(ms + NORM_EPS) * gn_ref[:, sl] * (z * jax.nn.sigmoid(z))
        y_ref[:, sl] = y.astype(BF16)


def _retention(p, ret_gn_row, n_seq, seq_len, C, init_state):
    n_chunks = seq_len // C
    has_init = init_state is not None

    def col(t):
        return pl.BlockSpec((C, COL_W), lambda b, c: (b * n_chunks + c, t))

    in_specs = [col(COL_QR), col(COL_KR), col(COL_VR), col(COL_ZR),
                pl.BlockSpec((1, COL_W), lambda b, c: (0, 0))]
    args = [p, p, p, p, ret_gn_row]
    state_spec = pl.BlockSpec((1, RET_HEADS, RET_DK, RET_DV), lambda b, c: (b, 0, 0, 0))
    if has_init:
        in_specs.append(state_spec)
        args.append(init_state)
    return pl.pallas_call(
        functools.partial(_ret_kernel, C=C, has_init=has_init),
        grid=(n_seq, n_chunks), in_specs=in_specs,
        out_specs=[pl.BlockSpec((C, COL_W), lambda b, c: (b * n_chunks + c, 0)), state_spec],
        out_shape=[jax.ShapeDtypeStruct((n_seq * seq_len, COL_W), BF16),
                   jax.ShapeDtypeStruct((n_seq, RET_HEADS, RET_DK, RET_DV), F32)],
        scratch_shapes=[pltpu.VMEM((RET_HEADS, C, C), F32),
                        pltpu.VMEM((RET_HEADS, C, RET_DK), F32),
                        pltpu.VMEM((RET_HEADS, C, RET_DK), F32)],
        compiler_params=_cparams(2), name="retention_init" if has_init else "retention",
    )(*args)


def _lambda_value(lq1, lk1, lq2, lk2, lam_init):
    a = jnp.sum(lq1[...] * lk1[...], axis=-1, keepdims=True)
    b = jnp.sum(lq2[...] * lk2[...], axis=-1, keepdims=True)
    return jnp.exp(a) - jnp.exp(b) + lam_init


def _bf16_split3(x):
    out, r = [], np.float64(x)
    for _ in range(3):
        c = np.float64(np.asarray(r, np.float32).astype(jnp.bfloat16).astype(np.float32))
        out.append(float(c))
        r = r - c
    return out


def _alibi_operands(blk):
    c = _bf16_split3(LOG2E)
    pos = np.arange(blk)
    ek = np.zeros((blk, 2 * DA_HEAD_DIM), np.float32)
    bq = np.zeros((2 * DA_HEAD_DIM, blk), np.float32)
    for t in range(3):
        ek[:, t] = pos >> 4
        ek[:, 3 + t] = pos & 15
        ek[:, 6 + t] = -16.0 * c[t]
        ek[:, 9 + t] = -c[t]
        bq[t, :] = 16.0 * c[t]
        bq[3 + t, :] = c[t]
        bq[6 + t, :] = pos >> 4
        bq[9 + t, :] = pos & 15
    return jnp.asarray(ek, BF16), jnp.asarray(bq, BF16)


def _attn_kernel(slog2_ref, spow2_ref, qt1_ref, qt2_ref, k1_ref, k2_ref, vt_ref, ek_ref, bq_ref,
                 lq1, lk1, lq2, lk2, gn_ref, o_ref,
                 qx_scr, m_scr, l_scr, acc_scr, s_scr, mb_scr, *, blk, lam_init):
    j = pl.program_id(1)
    qi = pl.program_id(2)
    half = DA_HEAD_DIM
    hd2 = 2 * DA_HEAD_DIM
    k_refs = (k1_ref, k2_ref)

    zeros = jnp.zeros((half, blk), BF16)
    for hh in range(2):
        bias_rows = (bq_ref[...].astype(F32) * spow2_ref[2 * j + hh]).astype(BF16)
        for c, qref in enumerate((qt1_ref, qt2_ref)):
            idx = 2 * c + hh
            if hh == 0:
                qx_scr[idx, :half, :] = qref[0, :half, :]
                qx_scr[idx, half:hd2, :] = zeros
            else:
                qx_scr[idx, :half, :] = zeros
                qx_scr[idx, half:hd2, :] = qref[0, half:, :]
            qx_scr[idx, hd2:, :] = bias_rows
    m_scr[...] = jnp.full(m_scr.shape, _NEG, F32)
    l_scr[...] = jnp.zeros(l_scr.shape, F32)
    acc_scr[...] = jnp.zeros(acc_scr.shape, F32)

    def score_diag(par):
        kk = lax.broadcasted_iota(jnp.int32, (blk, blk), 0)
        qq = lax.broadcasted_iota(jnp.int32, (blk, blk), 1)
        allowed = (kk // CHUNK) <= (qq // CHUNK)
        dist = jnp.abs(kk - qq).astype(F32)
        row0 = pl.multiple_of(qi * blk, blk)
        for c in range(2):
            kc = k_refs[c][pl.ds(row0, blk), :]
            for hh in range(2):
                idx = 2 * c + hh
                s = jnp.dot(kc, qx_scr[idx, :hd2, :], preferred_element_type=F32)
                s = jnp.where(allowed, s - slog2_ref[2 * j + hh] * dist, _NEG)
                s_scr[par, idx] = s
                mb_scr[par, idx] = jnp.max(s, axis=0, keepdims=True)

    def score_off(ki, par):
        row0 = pl.multiple_of(ki * blk, blk)
        for c in range(2):
            kx = jnp.concatenate([k_refs[c][pl.ds(row0, blk), :], ek_ref[...]], axis=1)
            for hh in range(2):
                idx = 2 * c + hh
                s = jnp.dot(kx, qx_scr[idx], preferred_element_type=F32)
                s_scr[par, idx] = s
                mb_scr[par, idx] = jnp.max(s, axis=0, keepdims=True)

    def consume(kb, par):
        dblk = ((qi - kb) * blk).astype(F32)
        for c in range(2):
            for hh in range(2):
                idx = 2 * c + hh
                shift = slog2_ref[2 * j + hh] * dblk
                m_old = m_scr[idx]
                m_new = jnp.maximum(m_old, mb_scr[par, idx] - shift)
                alpha = jnp.exp2(m_old - m_new)
                p = jnp.exp2(s_scr[par, idx] - (m_new + shift))
                l_scr[idx] = alpha * l_scr[idx] + jnp.sum(p, axis=0, keepdims=True)
                vt_h = vt_ref[kb, hh * hd2:(hh + 1) * hd2, :]
                acc_scr[idx] = alpha * acc_scr[idx] + jnp.dot(
                    vt_h, p.astype(BF16), preferred_element_type=F32)
                m_scr[idx] = m_new

    score_diag(0)

    def pair_body(u, carry):
        t0 = 2 * u
        score_off(t0, 1)
        consume(jnp.where(t0 == 0, qi, t0 - 1), 0)
        score_off(t0 + 1, 0)
        consume(t0, 1)
        return carry

    lax.fori_loop(0, qi // 2, pair_body, 0)

    @pl.when(qi % 2 == 1)
    def _():
        t0 = qi - 1
        score_off(t0, 1)
        consume(jnp.where(t0 == 0, qi, t0 - 1), 0)
        consume(t0, 1)

    @pl.when(qi % 2 == 0)
    def _():
        consume(jnp.where(qi == 0, qi, qi - 1), 0)

    lam = _lambda_value(lq1, lk1, lq2, lk2, lam_init)
    for hh in range(2):
        o = acc_scr[hh] / l_scr[hh] - lam * (acc_scr[2 + hh] / l_scr[2 + hh])
        ms = jnp.mean(o * o, axis=0, keepdims=True)
        on = (o * lax.rsqrt(ms + NORM_EPS)).T
        sl = slice(hh * hd2, (hh + 1) * hd2)
        o_ref[:, sl] = (on * gn_ref[:, sl] * (1.0 - lam_init)).astype(BF16)


def _attention_prompt(p, qt, vt, lam_rows, da_gn_row, n_seq, seq_len, blk, lam_init):
    nb = seq_len // blk
    n_hp = DA_HEADS // 2
    hd2 = 2 * DA_HEAD_DIM
    kcol0 = COL_KD * COL_W // hd2
    ek, bq = _alibi_operands(blk)

    smem = pl.BlockSpec(memory_space=pltpu.SMEM)
    in_specs = [
        smem, smem,
        pl.BlockSpec((1, hd2, blk), lambda b, j, q: (b * nb + q, j, 0)),
        pl.BlockSpec((1, hd2, blk), lambda b, j, q: (b * nb + q, n_hp + j, 0)),
        pl.BlockSpec((seq_len, hd2), lambda b, j, q: (b, kcol0 + j)),
        pl.BlockSpec((seq_len, hd2), lambda b, j, q: (b, kcol0 + n_hp + j)),
        pl.BlockSpec((nb, 2 * hd2, blk), lambda b, j, q: (b, j, 0)),
        pl.BlockSpec((blk, hd2), lambda b, j, q: (0, 0)),
        pl.BlockSpec((hd2, blk), lambda b, j, q: (0, 0)),
    ] + [pl.BlockSpec((1, DA_HEAD_DIM), lambda b, j, q: (0, 0))] * 4 + [
        pl.BlockSpec((1, 2 * hd2), lambda b, j, q: (0, j)),
    ]
    return pl.pallas_call(
        functools.partial(_attn_kernel, blk=blk, lam_init=lam_init),
        grid=(n_seq, n_hp, nb), in_specs=in_specs,
        out_specs=pl.BlockSpec((blk, 2 * hd2), lambda b, j, q: (b * nb + q, j)),
        out_shape=jax.ShapeDtypeStruct((n_seq * seq_len, COL_W), BF16),
        scratch_shapes=[pltpu.VMEM((4, 2 * hd2, blk), BF16),
                        pltpu.VMEM((4, 1, blk), F32),
                        pltpu.VMEM((4, 1, blk), F32),
                        pltpu.VMEM((4, hd2, blk), F32),
                        pltpu.VMEM((2, 4, blk, blk), F32),
                        pltpu.VMEM((2, 4, 1, blk), F32)],
        compiler_params=_cparams(3), name="diff_attn_prompt",
    )(jnp.asarray(_SLOPES_LOG2), jnp.asarray(_SLOPES), qt, qt, p, p, vt, ek, bq,
      *lam_rows, da_gn_row)


def _sattn_kernel(slopes_ref, q_ref, kn_ref, vn_ref, kc_ref, vc_ref,
                  lq1, lk1, lq2, lk2, gn_ref, o_ref, *, L, P, lam_init):
    lam = _lambda_value(lq1, lk1, lq2, lk2, lam_init)
    hd2 = 2 * DA_HEAD_DIM
    lane = lax.broadcasted_iota(jnp.int32, (L, hd2), 1)
    qpos_c = P + lax.broadcasted_iota(jnp.int32, (L, P), 0)
    kpos_c = lax.broadcasted_iota(jnp.int32, (L, P), 1)
    qpos_n = P + lax.broadcasted_iota(jnp.int32, (L, L), 0)
    kpos_n = P + lax.broadcasted_iota(jnp.int32, (L, L), 1)
    dist_c = jnp.abs(qpos_c - kpos_c).astype(F32)
    dist_n = jnp.abs(qpos_n - kpos_n).astype(F32)
    ok_c = (kpos_c // CHUNK) <= (qpos_c // CHUNK)
    ok_n = (kpos_n // CHUNK) <= (qpos_n // CHUNK)
    nt = (((1,), (1,)), ((), ()))
    for pr in range(DA_HEADS // 2):
        for hh in range(2):
            h = 2 * pr + hh
            slope = slopes_ref[h]
            probs = []
            for c in range(2):
                sl = slice((c * (DA_HEADS // 2) + pr) * hd2, (c * (DA_HEADS // 2) + pr + 1) * hd2)
                q128 = q_ref[:, sl].astype(F32) * _Q_SCALE_LOG2
                keep = (lane < DA_HEAD_DIM) if hh == 0 else (lane >= DA_HEAD_DIM)
                qm = jnp.where(keep, q128, 0.0).astype(BF16)
                kc_t = kc_ref[0, sl, :].astype(BF16)
                kn = kn_ref[:, sl]
                s_c = jnp.dot(qm, kc_t, preferred_element_type=F32)
                s_n = lax.dot_general(qm, kn, nt, preferred_element_type=F32)
                s_c = jnp.where(ok_c, s_c - slope * dist_c, _NEG)
                s_n = jnp.where(ok_n, s_n - slope * dist_n, _NEG)
                m = jnp.maximum(jnp.max(s_c, axis=-1, keepdims=True),
                                jnp.max(s_n, axis=-1, keepdims=True))
                p_c = jnp.exp2(s_c - m)
                p_n = jnp.exp2(s_n - m)
                inv = 1.0 / (jnp.sum(p_c, axis=-1, keepdims=True)
                             + jnp.sum(p_n, axis=-1, keepdims=True))
                probs.append((p_c * inv, p_n * inv))
            a_c = (probs[0][0] - lam * probs[1][0]).astype(BF16)
            a_n = (probs[0][1] - lam * probs[1][1]).astype(BF16)
            vsl = slice(h * hd2, (h + 1) * hd2)
            vc = vc_ref[0, pl.ds(h, P, stride=DA_HEADS), :].astype(BF16)
            o = (jnp.dot(a_c, vc, preferred_element_type=F32)
                 + jnp.dot(a_n, vn_ref[:, vsl], preferred_element_type=F32))
            ms = jnp.mean(o * o, axis=-1, keepdims=True)
            on = o * lax.rsqrt(ms + NORM_EPS) * gn_ref[:, vsl] * (1.0 - lam_init)
            o_ref[:, vsl] = on.astype(BF16)


def _attention_sample(p, cache_kt, cache_v, lam_rows, da_gn_row, n_seq, L, lam_init):
    P = cache_kt.shape[2]
    smem = pl.BlockSpec(memory_space=pltpu.SMEM)
    in_specs = [
        smem,
        pl.BlockSpec((L, COL_W), lambda s: (s, COL_QD)),
        pl.BlockSpec((L, COL_W), lambda s: (s, COL_KD)),
        pl.BlockSpec((L, COL_W), lambda s: (s, COL_VD)),
        pl.BlockSpec((1, COL_W, P), lambda s: (s, 0, 0)),
        pl.BlockSpec((1, P * DA_HEADS, RET_DV), lambda s: (s, 0, 0)),
    ] + [pl.BlockSpec((1, DA_HEAD_DIM), lambda s: (0, 0))] * 4 + [
        pl.BlockSpec((1, COL_W), lambda s: (0, 0)),
    ]
    return pl.pallas_call(
        functools.partial(_sattn_kernel, L=L, P=P, lam_init=lam_init),
        grid=(n_seq,), in_specs=in_specs,
        out_specs=pl.BlockSpec((L, COL_W), lambda s: (s, 0)),
        out_shape=jax.ShapeDtypeStruct((n_seq * L, COL_W), BF16),
        compiler_params=_cparams(1), name="diff_attn_sample",
    )(jnp.asarray(_SLOPES_LOG2), p, p, p, cache_kt, cache_v, *lam_rows, da_gn_row)


def _merge_kernel(x_ref, yr_ref, yd_ref, gr0, gr1, gd0, gd1, wr_ref, wd_ref, wo_ref, gm_ref,
                  x1_ref, h2_ref):
    yr = jnp.dot(yr_ref[...], wr_ref[...], preferred_element_type=F32)
    yd = jnp.dot(yd_ref[...], wd_ref[...], preferred_element_type=F32)
    gr = jnp.concatenate([gr0[...], gr1[...]], axis=1).astype(F32)
    gd = jnp.concatenate([gd0[...], gd1[...]], axis=1).astype(F32)
    m = jax.nn.sigmoid(gr) * yr + jax.nn.sigmoid(gd) * yd
    x1 = x_ref[...] + jnp.dot(m.astype(BF16), wo_ref[...], preferred_element_type=F32)
    x1_ref[...] = x1
    ms = jnp.mean(x1 * x1, axis=-1, keepdims=True)
    h2_ref[...] = (x1 * lax.rsqrt(ms + NORM_EPS) * gm_ref[...]).astype(BF16)


def _merge(x2d, yr_in, yd_in, p, w_ret_o, w_da_o, w_out, g_mlp_row, tm):
    T = x2d.shape[0]

    def whole(shape):
        return pl.BlockSpec(shape, lambda i: (0, 0), pipeline_mode=pl.Buffered(1))

    def pcol(t):
        return pl.BlockSpec((tm, COL_W), lambda i: (i, t))

    in_specs = [
        pl.BlockSpec((tm, D_MODEL), lambda i: (i, 0)),
        pl.BlockSpec((tm, COL_W), lambda i: (i, 0)),
        pl.BlockSpec((tm, COL_W), lambda i: (i, 0)),
        pcol(COL_GR), pcol(COL_GR + 1), pcol(COL_GD), pcol(COL_GD + 1),
        whole((COL_W, D_MODEL)), whole((COL_W, D_MODEL)), whole((D_MODEL, D_MODEL)),
        pl.BlockSpec((1, D_MODEL), lambda i: (0, 0)),
    ]
    return pl.pallas_call(
        _merge_kernel, grid=(T // tm,), in_specs=in_specs,
        out_specs=[pl.BlockSpec((tm, D_MODEL), lambda i: (i, 0)),
                   pl.BlockSpec((tm, D_MODEL), lambda i: (i, 0))],
        out_shape=[jax.ShapeDtypeStruct((T, D_MODEL), F32),
                   jax.ShapeDtypeStruct((T, D_MODEL), BF16)],
        compiler_params=_cparams(1), name="merge",
    )(x2d, yr_in, yd_in, p, p, p, p, w_ret_o, w_da_o, w_out, g_mlp_row)


def _mlp_kernel(x1_ref, h2_ref, wu_ref, wd_ref, y_ref, r_scr, *, n_steps, nf):
    t = pl.program_id(0)
    fb = (jnp.maximum(t, 1) - 1) % nf

    def up(par):
        u = jnp.dot(h2_ref[...], wu_ref[...], preferred_element_type=F32)
        r_scr[par] = jnp.square(jnp.maximum(u, 0.0)).astype(BF16)

    def down(par, first_chunk):
        contrib = jnp.dot(r_scr[par], wd_ref[...], preferred_element_type=F32)
        if first_chunk:
            y_ref[...] = x1_ref[...] + contrib
        else:
            y_ref[...] += contrib

    mid = (t > 0) & (t < n_steps - 1)

    @pl.when(t == 0)
    def _():
        up(0)

    @pl.when(mid & (t % 2 == 0))
    def _():
        down(1, False)
        up(0)

    @pl.when(mid & (t % 2 == 1) & (fb == 0))
    def _():
        down(0, True)
        up(1)

    @pl.when(mid & (t % 2 == 1) & (fb != 0))
    def _():
        down(0, False)
        up(1)

    @pl.when(t == n_steps - 1)
    def _():
        down((n_steps - 2) % 2, False)


def _mlp(x1, h2, w_up, w_down, tm, tf):
    T = x1.shape[0]
    nf = D_FF // tf
    assert nf % 2 == 0 and nf >= 2
    n_up = (T // tm) * nf
    n_steps = n_up + 1

    def up_idx(t):
        return jnp.minimum(t, n_up - 1)

    def down_idx(t):
        return jnp.maximum(t, 1) - 1

    return pl.pallas_call(
        functools.partial(_mlp_kernel, n_steps=n_steps, nf=nf), grid=(n_steps,),
        in_specs=[pl.BlockSpec((tm, D_MODEL), lambda t: (down_idx(t) // nf, 0)),
                  pl.BlockSpec((tm, D_MODEL), lambda t: (up_idx(t) // nf, 0)),
                  pl.BlockSpec((D_MODEL, tf), lambda t: (0, up_idx(t) % nf)),
                  pl.BlockSpec((tf, D_MODEL), lambda t: (down_idx(t) % nf, 0))],
        out_specs=pl.BlockSpec((tm, D_MODEL), lambda t: (down_idx(t) // nf, 0)),
        out_shape=jax.ShapeDtypeStruct((T, D_MODEL), F32),
        scratch_shapes=[pltpu.VMEM((2, tm, tf), BF16)],
        compiler_params=_cparams(1), name="mlp",
    )(x1, h2, w_up, w_down)


def _rope_tables(pos):
    half = RET_DK // 2
    inv = jnp.exp(-math.log(ROPE_BASE) * jnp.arange(half, dtype=F32) / half)
    ang = pos[:, None] * inv[None, :]
    cos = jnp.cos(ang)
    sin = jnp.sin(ang)
    return jnp.concatenate([cos, cos], axis=1), jnp.concatenate([-sin, sin], axis=1)


def _block_diag_mean():
    r = np.arange(V7X_MXU_DIM) // DA_HEAD_DIM
    return jnp.asarray((r[:, None] == r[None, :]).astype(np.float32) / DA_HEAD_DIM, BF16)


def kernel(x_prompt, x_sample, cache_da_k, cache_da_v, state_ret, g_mix, w_in, ret_gn, qn_g, kn_g,
           lambda_q1, lambda_k1, lambda_q2, lambda_k2, da_gn, w_ret_o, w_da_o, w_out, g_mlp,
           w_up, w_down):
    B, S, _ = x_prompt.shape
    DB, L, _ = x_sample.shape
    P = cache_da_k.shape[2]
    depth = w_in.shape[0]
    assert S % TOKEN_TILE == 0 and S % RET_CHUNK_PROMPT == 0 and L % 16 == 0

    tm_p = TOKEN_TILE
    tm_s = _pick_tile(DB * L, TOKEN_TILE)
    assert tm_s % L == 0 or L % tm_s == 0
    cos_p, sin_p = _rope_tables(jnp.arange(S, dtype=F32))
    cos_s, sin_s = _rope_tables(P + jnp.arange(L, dtype=F32))
    if tm_s > L:
        cos_s = jnp.tile(cos_s, (tm_s // L, 1))
        sin_s = jnp.tile(sin_s, (tm_s // L, 1))
    bd = _block_diag_mean()
    n_heads_qk = COL_W // DA_HEAD_DIM

    xp = x_prompt.reshape(B * S, D_MODEL)
    xs = x_sample.reshape(DB * L, D_MODEL)
    kp_l, vp_l, rp_l, ks_l, vs_l, rs_l = [], [], [], [], [], []
    for l in range(depth):
        lam_init = 0.8 - 0.6 * math.exp(-0.3 * l)
        w_in_bf = w_in[l].astype(BF16)
        w_ro, w_do, w_o = w_ret_o[l].astype(BF16), w_da_o[l].astype(BF16), w_out[l].astype(BF16)
        w_u, w_d = w_up[l].astype(BF16), w_down[l].astype(BF16)
        g_mix_row = g_mix[l].reshape(1, D_MODEL)
        g_mlp_row = g_mlp[l].reshape(1, D_MODEL)
        qg16 = jnp.tile(qn_g[l], n_heads_qk).reshape(1, COL_W)
        kg16 = jnp.tile(kn_g[l], n_heads_qk).reshape(1, COL_W)
        ret_gn_row = ret_gn[l].reshape(1, COL_W)
        da_gn_row = da_gn[l].reshape(1, COL_W)
        lam_rows = [a[l].astype(F32).reshape(1, DA_HEAD_DIM)
                    for a in (lambda_q1, lambda_k1, lambda_q2, lambda_k2)]

        p, nkt, nv, qt, vt = _inproj(_rmsnorm(xp, g_mix_row, tm_p), w_in_bf, cos_p, sin_p,
                                     qg16, kg16, bd, tm_p, S)
        yr_in, s_fin = _retention(p, ret_gn_row, B, S, RET_CHUNK_PROMPT, None)
        yd_in = _attention_prompt(p, qt, vt, lam_rows, da_gn_row, B, S, TOKEN_TILE, lam_init)
        x1, h2 = _merge(xp, yr_in, yd_in, p, w_ro, w_do, w_o, g_mlp_row, tm_p)
        xp = _mlp(x1, h2, w_u, w_d, tm_p, MLP_FF_TILE)
        nk = jnp.transpose(nkt.reshape(B, 2 * DA_HEADS, DA_HEAD_DIM, S), (0, 3, 1, 2))
        kp_l.append(nk)
        vp_l.append(nv.reshape(B, S, DA_HEADS, 2 * DA_HEAD_DIM))
        rp_l.append(s_fin)

        p, nk, nv = _inproj(_rmsnorm(xs, g_mix_row, tm_s), w_in_bf, cos_s, sin_s, qg16, kg16, bd,
                            tm_s, None)
        yr_in, s_new = _retention(p, ret_gn_row, DB, L, L, state_ret[l].astype(F32))
        cache_kt = jnp.transpose(cache_da_k[l].reshape(DB, P, COL_W), (0, 2, 1))
        yd_in = _attention_sample(p, cache_kt, cache_da_v[l].reshape(DB, P * DA_HEADS, RET_DV),
                                  lam_rows, da_gn_row, DB, L, lam_init)
        x1, h2 = _merge(xs, yr_in, yd_in, p, w_ro, w_do, w_o, g_mlp_row, tm_s)
        xs = _mlp(x1, h2, w_u, w_d, tm_s, MLP_FF_TILE)
        ks_l.append(nk.reshape(DB, L, 2 * DA_HEADS, DA_HEAD_DIM))
        vs_l.append(nv.reshape(DB, L, DA_HEADS, 2 * DA_HEAD_DIM))
        rs_l.append(s_new)

    return (xp.reshape(B, S, D_MODEL), xs.reshape(DB, L, D_MODEL),
            jnp.stack(kp_l, 0), jnp.stack(vp_l, 0), jnp.stack(rp_l, 0),
            jnp.stack(ks_l, 0), jnp.stack(vs_l, 0), jnp.stack(rs_l, 0))
```

```python
import functools
import math

import numpy as np
import jax
import jax.numpy as jnp
from jax import lax
from jax.experimental import pallas as pl
from jax.experimental.pallas import tpu as pltpu

D_MODEL = 2048
CHUNK = 64
RET_HEADS = 8
RET_DK = 128
RET_DV = 128
DA_HEADS = 8
DA_HEAD_DIM = 64
D_FF = 4 * D_MODEL
ROPE_BASE = 10000.0
NORM_EPS = 1e-6
LOG2E = 1.4426950408889634

V7X_LANES = 128
V7X_MXU_DIM = 256
V7X_VMEM_LIMIT_BYTES = 56 * 1024 * 1024

COL_W = 1024
N_COL_TILES = 11
COL_QR, COL_KR, COL_VR, COL_ZR, COL_QD, COL_KD, COL_VD = 0, 1, 2, 3, 4, 5, 6
COL_GR, COL_GD = 7, 9

TOKEN_TILE = 512
RET_CHUNK_PROMPT = 256
MLP_FF_TILE = 1024
Q_SPLIT = 2

F32 = jnp.float32
BF16 = jnp.bfloat16

_LOG_G = [math.log1p(-(2.0 ** (-5.0 - h))) for h in range(RET_HEADS)]
_SLOPES = np.asarray([2.0 ** (-8.0 * (h + 1) / DA_HEADS) for h in range(DA_HEADS)], np.float32)
_SLOPES_LOG2 = (_SLOPES * np.float32(LOG2E)).astype(np.float32)
_Q_SCALE_LOG2 = (DA_HEAD_DIM ** -0.5) * LOG2E
_NEG = -1e30


def _cparams(n_grid):
    return pltpu.CompilerParams(
        dimension_semantics=("arbitrary",) * n_grid,
        vmem_limit_bytes=V7X_VMEM_LIMIT_BYTES)


def _pick_tile(n, cap):
    t = cap
    while n % t:
        t //= 2
    return t


def _rmsnorm_kernel(x_ref, g_ref, h_ref):
    x = x_ref[...]
    ms = jnp.mean(x * x, axis=-1, keepdims=True)
    h_ref[...] = (x * lax.rsqrt(ms + NORM_EPS) * g_ref[...]).astype(BF16)


def _rmsnorm(x2d, g_row, tm):
    T = x2d.shape[0]
    return pl.pallas_call(
        _rmsnorm_kernel, grid=(T // tm,),
        in_specs=[pl.BlockSpec((tm, D_MODEL), lambda i: (i, 0)),
                  pl.BlockSpec((1, D_MODEL), lambda i: (0, 0))],
        out_specs=pl.BlockSpec((tm, D_MODEL), lambda i: (i, 0)),
        out_shape=jax.ShapeDtypeStruct((T, D_MODEL), BF16),
        compiler_params=_cparams(1), name="rmsnorm",
    )(x2d, g_row)


def _inproj_kernel(h_ref, w_ref, cos_ref, sin_ref, qg_ref, kg_ref, bd_ref,
                   p_ref, nk_ref, nv_ref, *rest, emit_t, n_steps, n_tok):
    if emit_t:
        qt_ref, vt_ref, s_scr = rest
    else:
        (s_scr,) = rest
    t = pl.program_id(0)
    jb = (jnp.maximum(t, 1) - 1) // n_tok

    def matmul():
        s_scr[...] = jnp.dot(h_ref[...], w_ref[...], preferred_element_type=F32)

    def rope(scale):
        cos = cos_ref[...]
        sin = sin_ref[...]
        for hh in range(RET_HEADS):
            sl = slice(hh * RET_DK, (hh + 1) * RET_DK)
            a = s_scr[:, sl]
            o = a * cos + pltpu.roll(a, RET_DK // 2, axis=1) * sin
            if scale != 1.0:
                o = o * scale
            p_ref[:, sl] = o.astype(BF16)

    def plain():
        p_ref[...] = s_scr[...].astype(BF16)

    def head_norm(acc, gain_row):
        sq = (acc * acc).astype(BF16)
        parts = [
            jnp.dot(sq[:, c * V7X_MXU_DIM:(c + 1) * V7X_MXU_DIM], bd_ref[...],
                    preferred_element_type=F32)
            for c in range(COL_W // V7X_MXU_DIM)
        ]
        ms = jnp.concatenate(parts, axis=1)
        return acc * lax.rsqrt(ms + NORM_EPS) * gain_row

    def q_diff():
        n = head_norm(s_scr[...], qg_ref[...])
        p_ref[...] = n.astype(BF16)
        if emit_t:
            qt_ref[0] = (n * _Q_SCALE_LOG2).T.astype(BF16)

    def k_diff():
        n = head_norm(s_scr[...], kg_ref[...])
        if emit_t:
            nk_ref[0] = n.T
        else:
            nk_ref[...] = n
        p_ref[...] = n.astype(BF16)

    def v_diff():
        acc = s_scr[...]
        for h in range(DA_HEADS):
            nv_ref[pl.ds(h, acc.shape[0], stride=DA_HEADS), :] = acc[:, h * RET_DV:(h + 1) * RET_DV]
        p_ref[...] = acc.astype(BF16)
        if emit_t:
            vt_ref[0] = acc.T.astype(BF16)

    epilogues = {COL_QR: lambda: rope(1.0), COL_KR: lambda: rope(RET_DK ** -0.5),
                 COL_QD: q_diff, COL_KD: k_diff, COL_VD: v_diff}
    mid = (t > 0) & (t < n_steps - 1)

    @pl.when(t == 0)
    def _():
        matmul()

    for col, epilogue in epilogues.items():
        @pl.when(mid & (jb == col))
        def _(epilogue=epilogue):
            epilogue()
            matmul()

    @pl.when(mid & ((jb == COL_VR) | (jb == COL_ZR) | (jb >= COL_GR)))
    def _():
        plain()
        matmul()

    @pl.when(t == n_steps - 1)
    def _():
        plain()


def _inproj(h2d, w_bf, cos_tab, sin_tab, qg16, kg16, bd, tm, seq_len):
    emit_t = seq_len is not None
    T = h2d.shape[0]
    n_tok = T // tm
    n_rope = cos_tab.shape[0] // tm
    n_mm = n_tok * N_COL_TILES
    n_steps = n_mm + 1

    def mm_tile(t):
        tt = jnp.minimum(t, n_mm - 1)
        return tt // n_tok, tt % n_tok

    def ep_tile(t):
        tt = jnp.maximum(t, 1) - 1
        return tt // n_tok, tt % n_tok

    def side_tile(t, col):
        j, i = ep_tile(t)
        return jnp.where(j < col, 0, jnp.where(j > col, n_tok - 1, i))

    in_specs = [
        pl.BlockSpec((tm, D_MODEL), lambda t: (mm_tile(t)[1], 0)),
        pl.BlockSpec((D_MODEL, COL_W), lambda t: (0, mm_tile(t)[0])),
        pl.BlockSpec((tm, RET_DK), lambda t: (ep_tile(t)[1] % n_rope, 0)),
        pl.BlockSpec((tm, RET_DK), lambda t: (ep_tile(t)[1] % n_rope, 0)),
        pl.BlockSpec((1, COL_W), lambda t: (0, 0)),
        pl.BlockSpec((1, COL_W), lambda t: (0, 0)),
        pl.BlockSpec((V7X_MXU_DIM, V7X_MXU_DIM), lambda t: (0, 0)),
    ]
    if emit_t:
        spb = seq_len // tm
        nk_shape = jax.ShapeDtypeStruct((T // seq_len, COL_W, seq_len), F32)
        nk_spec = pl.BlockSpec((1, COL_W, tm), lambda t: (side_tile(t, COL_KD) // spb, 0,
                                                          side_tile(t, COL_KD) % spb))
    else:
        nk_shape = jax.ShapeDtypeStruct((T, COL_W), F32)
        nk_spec = pl.BlockSpec((tm, COL_W), lambda t: (side_tile(t, COL_KD), 0))
    out_shape = [
        jax.ShapeDtypeStruct((T, N_COL_TILES * COL_W), BF16),
        nk_shape,
        jax.ShapeDtypeStruct((T * DA_HEADS, RET_DV), F32),
    ]
    out_specs = [
        pl.BlockSpec((tm, COL_W), lambda t: (ep_tile(t)[1], ep_tile(t)[0])),
        nk_spec,
        pl.BlockSpec((tm * DA_HEADS, RET_DV), lambda t: (side_tile(t, COL_VD), 0)),
    ]
    if emit_t:
        out_shape += [jax.ShapeDtypeStruct((T // tm, COL_W, tm), BF16)] * 2
        out_specs += [pl.BlockSpec((1, COL_W, tm), lambda t: (side_tile(t, COL_QD), 0, 0)),
                      pl.BlockSpec((1, COL_W, tm), lambda t: (side_tile(t, COL_VD), 0, 0))]
    return pl.pallas_call(
        functools.partial(_inproj_kernel, emit_t=emit_t, n_steps=n_steps, n_tok=n_tok),
        grid=(n_steps,), in_specs=in_specs, out_specs=out_specs, out_shape=out_shape,
        scratch_shapes=[pltpu.VMEM((tm, COL_W), F32)],
        compiler_params=_cparams(1), name="inproj_t" if emit_t else "inproj",
    )(h2d, w_bf, cos_tab, sin_tab, qg16, kg16, bd)


def _ret_kernel(q_ref, k_ref, v_ref, z_ref, gn_ref, *rest, C, has_init):
    if has_init:
        s0_ref, y_ref, s_ref, dm_scr, rd_scr, kd_scr = rest
    else:
        y_ref, s_ref, dm_scr, rd_scr, kd_scr = rest
    b = pl.program_id(0)
    c = pl.program_id(1)

    @pl.when((b == 0) & (c == 0))
    def _():
        li = lax.broadcasted_iota(jnp.int32, (C, C), 0)
        mi = lax.broadcasted_iota(jnp.int32, (C, C), 1)
        causal = li >= mi
        diff = jnp.where(causal, (li - mi).astype(F32), 0.0)
        row = lax.broadcasted_iota(jnp.int32, (C, RET_DK), 0).astype(F32)
        for h in range(RET_HEADS):
            dm_scr[h] = jnp.where(causal, jnp.exp(diff * _LOG_G[h]), 0.0)
            rd_scr[h] = jnp.exp((row + 1.0) * _LOG_G[h])
            kd_scr[h] = jnp.exp((C - 1.0 - row) * _LOG_G[h])

    @pl.when(c == 0)
    def _():
        if has_init:
            s_ref[...] = s0_ref[...]
        else:
            s_ref[...] = jnp.zeros(s_ref.shape, F32)

    for h in range(RET_HEADS):
        sl = slice(h * RET_DK, (h + 1) * RET_DK)
        q = q_ref[:, sl]
        k = k_ref[:, sl]
        v = v_ref[:, sl]
        state = s_ref[0, h]
        sc = lax.dot_general(q, k, (((1,), (1,)), ((), ())), preferred_element_type=F32)
        sc = sc * dm_scr[h]
        intra = jnp.dot(sc.astype(BF16), v, preferred_element_type=F32)
        inter = jnp.dot(q, state.astype(BF16), preferred_element_type=F32) * rd_scr[h]
        o = intra + inter
        kw = (k.astype(F32) * kd_scr[h]).astype(BF16)
        upd = lax.dot_general(kw, v, (((0,), (0,)), ((), ())), preferred_element_type=F32)
        s_ref[0, h] = state * math.exp(C * _LOG_G[h]) + upd
        ms = jnp.mean(o * o, axis=-1, keepdims=True)
        z = z_ref[:, sl].astype(F32)
        y = o * lax.rsqrt(ms + NORM_EPS) * gn_ref[:, sl] * (z * jax.nn.sigmoid(z))
        y_ref[:, sl] = y.astype(BF16)


def _retention(p, ret_gn_row, n_seq, seq_len, C, init_state):
    n_chunks = seq_len // C
    has_init = init_state is not None

    def col(t):
        return pl.BlockSpec((C, COL_W), lambda b, c: (b * n_chunks + c, t))

    in_specs = [col(COL_QR), col(COL_KR), col(COL_VR), col(COL_ZR),
                pl.BlockSpec((1, COL_W), lambda b, c: (0, 0))]
    args = [p, p, p, p, ret_gn_row]
    state_spec = pl.BlockSpec((1, RET_HEADS, RET_DK, RET_DV), lambda b, c: (b, 0, 0, 0))
    if has_init:
        in_specs.append(state_spec)
        args.append(init_state)
    return pl.pallas_call(
        functools.partial(_ret_kernel, C=C, has_init=has_init),
        grid=(n_seq, n_chunks), in_specs=in_specs,
        out_specs=[pl.BlockSpec((C, COL_W), lambda b, c: (b * n_chunks + c, 0)), state_spec],
        out_shape=[jax.ShapeDtypeStruct((n_seq * seq_len, COL_W), BF16),
                   jax.ShapeDtypeStruct((n_seq, RET_HEADS, RET_DK, RET_DV), F32)],
        scratch_shapes=[pltpu.VMEM((RET_HEADS, C, C), F32),
                        pltpu.VMEM((RET_HEADS, C, RET_DK), F32),
                        pltpu.VMEM((RET_HEADS, C, RET_DK), F32)],
        compiler_params=_cparams(2), name="retention_init" if has_init else "retention",
    )(*args)


def _lambda_value(lq1, lk1, lq2, lk2, lam_init):
    a = jnp.sum(lq1[...] * lk1[...], axis=-1, keepdims=True)
    b = jnp.sum(lq2[...] * lk2[...], axis=-1, keepdims=True)
    return jnp.exp(a) - jnp.exp(b) + lam_init


def _bf16_split3(x):
    out, r = [], np.float64(x)
    for _ in range(3):
        c = np.float64(np.asarray(r, np.float32).astype(jnp.bfloat16).astype(np.float32))
        out.append(float(c))
        r = r - c
    return out


def _alibi_operands(blk):
    c = _bf16_split3(LOG2E)
    pos = np.arange(blk)
    ek = np.zeros((blk, 2 * DA_HEAD_DIM), np.float32)
    bq = np.zeros((2 * DA_HEAD_DIM, blk), np.float32)
    for t in range(3):
        ek[:, t] = pos >> 4
        ek[:, 3 + t] = pos & 15
        ek[:, 6 + t] = -16.0 * c[t]
        ek[:, 9 + t] = -c[t]
        bq[t, :] = 16.0 * c[t]
        bq[3 + t, :] = c[t]
        bq[6 + t, :] = pos >> 4
        bq[9 + t, :] = pos & 15
    return jnp.asarray(ek, BF16), jnp.asarray(bq, BF16)


def _attn_kernel(slog2_ref, spow2_ref, qt1_ref, qt2_ref, k1_ref, k2_ref, vt_ref, ek_ref, bq_ref,
                 lq1, lk1, lq2, lk2, gn_ref, o_ref,
                 qx_scr, m_scr, l_scr, acc_scr, s_scr, mb_scr, *, blk, lam_init):
    j = pl.program_id(1)
    qi = pl.program_id(2)
    half = DA_HEAD_DIM
    hd2 = 2 * DA_HEAD_DIM
    k_refs = (k1_ref, k2_ref)

    zeros = jnp.zeros((half, blk), BF16)
    for hh in range(2):
        bias_rows = (bq_ref[...].astype(F32) * spow2_ref[2 * j + hh]).astype(BF16)
        for c, qref in enumerate((qt1_ref, qt2_ref)):
            idx = 2 * c + hh
            if hh == 0:
                qx_scr[idx, :half, :] = qref[0, :half, :]
                qx_scr[idx, half:hd2, :] = zeros
            else:
                qx_scr[idx, :half, :] = zeros
                qx_scr[idx, half:hd2, :] = qref[0, half:, :]
            qx_scr[idx, hd2:, :] = bias_rows
    m_scr[...] = jnp.full(m_scr.shape, _NEG, F32)
    l_scr[...] = jnp.zeros(l_scr.shape, F32)
    acc_scr[...] = jnp.zeros(acc_scr.shape, F32)

    def score_diag(par):
        kk = lax.broadcasted_iota(jnp.int32, (blk, blk), 0)
        qq = lax.broadcasted_iota(jnp.int32, (blk, blk), 1)
        allowed = (kk // CHUNK) <= (qq // CHUNK)
        dist = jnp.abs(kk - qq).astype(F32)
        row0 = pl.multiple_of(qi * blk, blk)
        for c in range(2):
            kc = k_refs[c][pl.ds(row0, blk), :]
            for hh in range(2):
                idx = 2 * c + hh
                s = jnp.dot(kc, qx_scr[idx, :hd2, :], preferred_element_type=F32)
                s = jnp.where(allowed, s - slog2_ref[2 * j + hh] * dist, _NEG)
                s_scr[par, idx] = s
                mb_scr[par, idx] = jnp.max(s, axis=0, keepdims=True)

    qw = blk // Q_SPLIT
    tiles = [(idx, pl.ds(h * qw, qw)) for idx in range(4) for h in range(Q_SPLIT)]

    def score_tile(ki, par, idx, qs):
        row0 = pl.multiple_of(ki * blk, blk)
        kx = jnp.concatenate([k_refs[idx // 2][pl.ds(row0, blk), :], ek_ref[...]], axis=1)
        s = jnp.dot(kx, qx_scr[idx, :, qs], preferred_element_type=F32)
        s_scr[par, idx, :, qs] = s
        mb_scr[par, idx, :, qs] = jnp.max(s, axis=0, keepdims=True)

    def consume_tile(kb, par, idx, qs):
        hh = idx % 2
        shift = slog2_ref[2 * j + hh] * ((qi - kb) * blk).astype(F32)
        m_old = m_scr[idx, :, qs]
        m_new = jnp.maximum(m_old, mb_scr[par, idx, :, qs] - shift)
        alpha = jnp.exp2(m_old - m_new)
        p = jnp.exp2(s_scr[par, idx, :, qs] - (m_new + shift))
        l_scr[idx, :, qs] = alpha * l_scr[idx, :, qs] + jnp.sum(p, axis=0, keepdims=True)
        vt_h = vt_ref[kb, hh * hd2:(hh + 1) * hd2, :]
        acc_scr[idx, :, qs] = alpha * acc_scr[idx, :, qs] + jnp.dot(
            vt_h, p.astype(BF16), preferred_element_type=F32)
        m_scr[idx, :, qs] = m_new

    def consume(kb, par):
        for idx, qs in tiles:
            consume_tile(kb, par, idx, qs)

    def stage(ki, par, kb):
        for idx, qs in tiles:
            score_tile(ki, par, idx, qs)
            consume_tile(kb, 1 - par, idx, qs)

    score_diag(0)

    def pair_body(u, carry):
        t0 = 2 * u
        stage(t0, 1, jnp.where(t0 == 0, qi, t0 - 1))
        stage(t0 + 1, 0, t0)
        return carry

    lax.fori_loop(0, qi // 2, pair_body, 0)

    @pl.when(qi % 2 == 1)
    def _():
        t0 = qi - 1
        stage(t0, 1, jnp.where(t0 == 0, qi, t0 - 1))
        consume(t0, 1)

    @pl.when(qi % 2 == 0)
    def _():
        consume(jnp.where(qi == 0, qi, qi - 1), 0)

    lam = _lambda_value(lq1, lk1, lq2, lk2, lam_init)
    for hh in range(2):
        o = acc_scr[hh] / l_scr[hh] - lam * (acc_scr[2 + hh] / l_scr[2 + hh])
        ms = jnp.mean(o * o, axis=0, keepdims=True)
        on = (o * lax.rsqrt(ms + NORM_EPS)).T
        sl = slice(hh * hd2, (hh + 1) * hd2)
        o_ref[:, sl] = (on * gn_ref[:, sl] * (1.0 - lam_init)).astype(BF16)


def _attention_prompt(p, qt, vt, lam_rows, da_gn_row, n_seq, seq_len, blk, lam_init):
    nb = seq_len // blk
    n_hp = DA_HEADS // 2
    hd2 = 2 * DA_HEAD_DIM
    kcol0 = COL_KD * COL_W // hd2
    ek, bq = _alibi_operands(blk)

    smem = pl.BlockSpec(memory_space=pltpu.SMEM)
    in_specs = [
        smem, smem,
        pl.BlockSpec((1, hd2, blk), lambda b, j, q: (b * nb + q, j, 0)),
        pl.BlockSpec((1, hd2, blk), lambda b, j, q: (b * nb + q, n_hp + j, 0)),
        pl.BlockSpec((seq_len, hd2), lambda b, j, q: (b, kcol0 + j)),
        pl.BlockSpec((seq_len, hd2), lambda b, j, q: (b, kcol0 + n_hp + j)),
        pl.BlockSpec((nb, 2 * hd2, blk), lambda b, j, q: (b, j, 0)),
        pl.BlockSpec((blk, hd2), lambda b, j, q: (0, 0)),
        pl.BlockSpec((hd2, blk), lambda b, j, q: (0, 0)),
    ] + [pl.BlockSpec((1, DA_HEAD_DIM), lambda b, j, q: (0, 0))] * 4 + [
        pl.BlockSpec((1, 2 * hd2), lambda b, j, q: (0, j)),
    ]
    return pl.pallas_call(
        functools.partial(_attn_kernel, blk=blk, lam_init=lam_init),
        grid=(n_seq, n_hp, nb), in_specs=in_specs,
        out_specs=pl.BlockSpec((blk, 2 * hd2), lambda b, j, q: (b * nb + q, j)),
        out_shape=jax.ShapeDtypeStruct((n_seq * seq_len, COL_W), BF16),
        scratch_shapes=[pltpu.VMEM((4, 2 * hd2, blk), BF16),
                        pltpu.VMEM((4, 1, blk), F32),
                        pltpu.VMEM((4, 1, blk), F32),
                        pltpu.VMEM((4, hd2, blk), F32),
                        pltpu.VMEM((2, 4, blk, blk), F32),
                        pltpu.VMEM((2, 4, 1, blk), F32)],
        compiler_params=_cparams(3), name="diff_attn_prompt",
    )(jnp.asarray(_SLOPES_LOG2), jnp.asarray(_SLOPES), qt, qt, p, p, vt, ek, bq,
      *lam_rows, da_gn_row)


def _sattn_kernel(slopes_ref, q_ref, kn_ref, vn_ref, kc_ref, vc_ref,
                  lq1, lk1, lq2, lk2, gn_ref, o_ref, *, L, P, lam_init):
    lam = _lambda_value(lq1, lk1, lq2, lk2, lam_init)
    hd2 = 2 * DA_HEAD_DIM
    lane = lax.broadcasted_iota(jnp.int32, (L, hd2), 1)
    qpos_c = P + lax.broadcasted_iota(jnp.int32, (L, P), 0)
    kpos_c = lax.broadcasted_iota(jnp.int32, (L, P), 1)
    qpos_n = P + lax.broadcasted_iota(jnp.int32, (L, L), 0)
    kpos_n = P + lax.broadcasted_iota(jnp.int32, (L, L), 1)
    dist_c = jnp.abs(qpos_c - kpos_c).astype(F32)
    dist_n = jnp.abs(qpos_n - kpos_n).astype(F32)
    ok_c = (kpos_c // CHUNK) <= (qpos_c // CHUNK)
    ok_n = (kpos_n // CHUNK) <= (qpos_n // CHUNK)
    nt = (((1,), (1,)), ((), ()))
    for pr in range(DA_HEADS // 2):
        for hh in range(2):
            h = 2 * pr + hh
            slope = slopes_ref[h]
            probs = []
            for c in range(2):
                sl = slice((c * (DA_HEADS // 2) + pr) * hd2, (c * (DA_HEADS // 2) + pr + 1) * hd2)
                q128 = q_ref[:, sl].astype(F32) * _Q_SCALE_LOG2
                keep = (lane < DA_HEAD_DIM) if hh == 0 else (lane >= DA_HEAD_DIM)
                qm = jnp.where(keep, q128, 0.0).astype(BF16)
                kc_t = kc_ref[0, sl, :].astype(BF16)
                kn = kn_ref[:, sl]
                s_c = jnp.dot(qm, kc_t, preferred_element_type=F32)
                s_n = lax.dot_general(qm, kn, nt, preferred_element_type=F32)
                s_c = jnp.where(ok_c, s_c - slope * dist_c, _NEG)
                s_n = jnp.where(ok_n, s_n - slope * dist_n, _NEG)
                m = jnp.maximum(jnp.max(s_c, axis=-1, keepdims=True),
                                jnp.max(s_n, axis=-1, keepdims=True))
                p_c = jnp.exp2(s_c - m)
                p_n = jnp.exp2(s_n - m)
                inv = 1.0 / (jnp.sum(p_c, axis=-1, keepdims=True)
                             + jnp.sum(p_n, axis=-1, keepdims=True))
                probs.append((p_c * inv, p_n * inv))
            a_c = (probs[0][0] - lam * probs[1][0]).astype(BF16)
            a_n = (probs[0][1] - lam * probs[1][1]).astype(BF16)
            vsl = slice(h * hd2, (h + 1) * hd2)
            vc = vc_ref[0, pl.ds(h, P, stride=DA_HEADS), :].astype(BF16)
            o = (jnp.dot(a_c, vc, preferred_element_type=F32)
                 + jnp.dot(a_n, vn_ref[:, vsl], preferred_element_type=F32))
            ms = jnp.mean(o * o, axis=-1, keepdims=True)
            on = o * lax.rsqrt(ms + NORM_EPS) * gn_ref[:, vsl] * (1.0 - lam_init)
            o_ref[:, vsl] = on.astype(BF16)


def _attention_sample(p, cache_kt, cache_v, lam_rows, da_gn_row, n_seq, L, lam_init):
    P = cache_kt.shape[2]
    smem = pl.BlockSpec(memory_space=pltpu.SMEM)
    in_specs = [
        smem,
        pl.BlockSpec((L, COL_W), lambda s: (s, COL_QD)),
        pl.BlockSpec((L, COL_W), lambda s: (s, COL_KD)),
        pl.BlockSpec((L, COL_W), lambda s: (s, COL_VD)),
        pl.BlockSpec((1, COL_W, P), lambda s: (s, 0, 0)),
        pl.BlockSpec((1, P * DA_HEADS, RET_DV), lambda s: (s, 0, 0)),
    ] + [pl.BlockSpec((1, DA_HEAD_DIM), lambda s: (0, 0))] * 4 + [
        pl.BlockSpec((1, COL_W), lambda s: (0, 0)),
    ]
    return pl.pallas_call(
        functools.partial(_sattn_kernel, L=L, P=P, lam_init=lam_init),
        grid=(n_seq,), in_specs=in_specs,
        out_specs=pl.BlockSpec((L, COL_W), lambda s: (s, 0)),
        out_shape=jax.ShapeDtypeStruct((n_seq * L, COL_W), BF16),
        compiler_params=_cparams(1), name="diff_attn_sample",
    )(jnp.asarray(_SLOPES_LOG2), p, p, p, cache_kt, cache_v, *lam_rows, da_gn_row)


def _merge_kernel(x_ref, yr_ref, yd_ref, gr0, gr1, gd0, gd1, wr_ref, wd_ref, wo_ref, gm_ref,
                  x1_ref, h2_ref):
    yr = jnp.dot(yr_ref[...], wr_ref[...], preferred_element_type=F32)
    yd = jnp.dot(yd_ref[...], wd_ref[...], preferred_element_type=F32)
    gr = jnp.concatenate([gr0[...], gr1[...]], axis=1).astype(F32)
    gd = jnp.concatenate([gd0[...], gd1[...]], axis=1).astype(F32)
    m = jax.nn.sigmoid(gr) * yr + jax.nn.sigmoid(gd) * yd
    x1 = x_ref[...] + jnp.dot(m.astype(BF16), wo_ref[...], preferred_element_type=F32)
    x1_ref[...] = x1
    ms = jnp.mean(x1 * x1, axis=-1, keepdims=True)
    h2_ref[...] = (x1 * lax.rsqrt(ms + NORM_EPS) * gm_ref[...]).astype(BF16)


def _merge(x2d, yr_in, yd_in, p, w_ret_o, w_da_o, w_out, g_mlp_row, tm):
    T = x2d.shape[0]

    def whole(shape):
        return pl.BlockSpec(shape, lambda i: (0, 0), pipeline_mode=pl.Buffered(1))

    def pcol(t):
        return pl.BlockSpec((tm, COL_W), lambda i: (i, t))

    in_specs = [
        pl.BlockSpec((tm, D_MODEL), lambda i: (i, 0)),
        pl.BlockSpec((tm, COL_W), lambda i: (i, 0)),
        pl.BlockSpec((tm, COL_W), lambda i: (i, 0)),
        pcol(COL_GR), pcol(COL_GR + 1), pcol(COL_GD), pcol(COL_GD + 1),
        whole((COL_W, D_MODEL)), whole((COL_W, D_MODEL)), whole((D_MODEL, D_MODEL)),
        pl.BlockSpec((1, D_MODEL), lambda i: (0, 0)),
    ]
    return pl.pallas_call(
        _merge_kernel, grid=(T // tm,), in_specs=in_specs,
        out_specs=[pl.BlockSpec((tm, D_MODEL), lambda i: (i, 0)),
                   pl.BlockSpec((tm, D_MODEL), lambda i: (i, 0))],
        out_shape=[jax.ShapeDtypeStruct((T, D_MODEL), F32),
                   jax.ShapeDtypeStruct((T, D_MODEL), BF16)],
        compiler_params=_cparams(1), name="merge",
    )(x2d, yr_in, yd_in, p, p, p, p, w_ret_o, w_da_o, w_out, g_mlp_row)


def _mlp_kernel(x1_ref, h2_ref, wu_ref, wd_ref, y_ref, r_scr, *, n_steps, nf):
    t = pl.program_id(0)
    fb = (jnp.maximum(t, 1) - 1) % nf

    def up(par):
        u = jnp.dot(h2_ref[...], wu_ref[...], preferred_element_type=F32)
        r_scr[par] = jnp.square(jnp.maximum(u, 0.0)).astype(BF16)

    def down(par, first_chunk):
        contrib = jnp.dot(r_scr[par], wd_ref[...], preferred_element_type=F32)
        if first_chunk:
            y_ref[...] = x1_ref[...] + contrib
        else:
            y_ref[...] += contrib

    mid = (t > 0) & (t < n_steps - 1)

    @pl.when(t == 0)
    def _():
        up(0)

    @pl.when(mid & (t % 2 == 0))
    def _():
        down(1, False)
        up(0)

    @pl.when(mid & (t % 2 == 1) & (fb == 0))
    def _():
        down(0, True)
        up(1)

    @pl.when(mid & (t % 2 == 1) & (fb != 0))
    def _():
        down(0, False)
        up(1)

    @pl.when(t == n_steps - 1)
    def _():
        down((n_steps - 2) % 2, False)


def _mlp(x1, h2, w_up, w_down, tm, tf):
    T = x1.shape[0]
    nf = D_FF // tf
    assert nf % 2 == 0 and nf >= 2
    n_up = (T // tm) * nf
    n_steps = n_up + 1

    def up_idx(t):
        return jnp.minimum(t, n_up - 1)

    def down_idx(t):
        return jnp.maximum(t, 1) - 1

    return pl.pallas_call(
        functools.partial(_mlp_kernel, n_steps=n_steps, nf=nf), grid=(n_steps,),
        in_specs=[pl.BlockSpec((tm, D_MODEL), lambda t: (down_idx(t) // nf, 0)),
                  pl.BlockSpec((tm, D_MODEL), lambda t: (up_idx(t) // nf, 0)),
                  pl.BlockSpec((D_MODEL, tf), lambda t: (0, up_idx(t) % nf)),
                  pl.BlockSpec((tf, D_MODEL), lambda t: (down_idx(t) % nf, 0))],
        out_specs=pl.BlockSpec((tm, D_MODEL), lambda t: (down_idx(t) // nf, 0)),
        out_shape=jax.ShapeDtypeStruct((T, D_MODEL), F32),
        scratch_shapes=[pltpu.VMEM((2, tm, tf), BF16)],
        compiler_params=_cparams(1), name="mlp",
    )(x1, h2, w_up, w_down)


def _rope_tables(pos):
    half = RET_DK // 2
    inv = jnp.exp(-math.log(ROPE_BASE) * jnp.arange(half, dtype=F32) / half)
    ang = pos[:, None] * inv[None, :]
    cos = jnp.cos(ang)
    sin = jnp.sin(ang)
    return jnp.concatenate([cos, cos], axis=1), jnp.concatenate([-sin, sin], axis=1)


def _block_diag_mean():
    r = np.arange(V7X_MXU_DIM) // DA_HEAD_DIM
    return jnp.asarray((r[:, None] == r[None, :]).astype(np.float32) / DA_HEAD_DIM, BF16)


def kernel(x_prompt, x_sample, cache_da_k, cache_da_v, state_ret, g_mix, w_in, ret_gn, qn_g, kn_g,
           lambda_q1, lambda_k1, lambda_q2, lambda_k2, da_gn, w_ret_o, w_da_o, w_out, g_mlp,
           w_up, w_down):
    B, S, _ = x_prompt.shape
    DB, L, _ = x_sample.shape
    P = cache_da_k.shape[2]
    depth = w_in.shape[0]
    assert S % TOKEN_TILE == 0 and S % RET_CHUNK_PROMPT == 0 and L % 16 == 0

    tm_p = TOKEN_TILE
    tm_s = _pick_tile(DB * L, TOKEN_TILE)
    assert tm_s % L == 0 or L % tm_s == 0
    cos_p, sin_p = _rope_tables(jnp.arange(S, dtype=F32))
    cos_s, sin_s = _rope_tables(P + jnp.arange(L, dtype=F32))
    if tm_s > L:
        cos_s = jnp.tile(cos_s, (tm_s // L, 1))
        sin_s = jnp.tile(sin_s, (tm_s // L, 1))
    bd = _block_diag_mean()
    n_heads_qk = COL_W // DA_HEAD_DIM

    xp = x_prompt.reshape(B * S, D_MODEL)
    xs = x_sample.reshape(DB * L, D_MODEL)
    kp_l, vp_l, rp_l, ks_l, vs_l, rs_l = [], [], [], [], [], []
    for l in range(depth):
        lam_init = 0.8 - 0.6 * math.exp(-0.3 * l)
        w_in_bf = w_in[l].astype(BF16)
        w_ro, w_do, w_o = w_ret_o[l].astype(BF16), w_da_o[l].astype(BF16), w_out[l].astype(BF16)
        w_u, w_d = w_up[l].astype(BF16), w_down[l].astype(BF16)
        g_mix_row = g_mix[l].reshape(1, D_MODEL)
        g_mlp_row = g_mlp[l].reshape(1, D_MODEL)
        qg16 = jnp.tile(qn_g[l], n_heads_qk).reshape(1, COL_W)
        kg16 = jnp.tile(kn_g[l], n_heads_qk).reshape(1, COL_W)
        ret_gn_row = ret_gn[l].reshape(1, COL_W)
        da_gn_row = da_gn[l].reshape(1, COL_W)
        lam_rows = [a[l].astype(F32).reshape(1, DA_HEAD_DIM)
                    for a in (lambda_q1, lambda_k1, lambda_q2, lambda_k2)]

        p, nkt, nv, qt, vt = _inproj(_rmsnorm(xp, g_mix_row, tm_p), w_in_bf, cos_p, sin_p,
                                     qg16, kg16, bd, tm_p, S)
        yr_in, s_fin = _retention(p, ret_gn_row, B, S, RET_CHUNK_PROMPT, None)
        yd_in = _attention_prompt(p, qt, vt, lam_rows, da_gn_row, B, S, TOKEN_TILE, lam_init)
        x1, h2 = _merge(xp, yr_in, yd_in, p, w_ro, w_do, w_o, g_mlp_row, tm_p)
        xp = _mlp(x1, h2, w_u, w_d, tm_p, MLP_FF_TILE)
        nk = jnp.transpose(nkt.reshape(B, 2 * DA_HEADS, DA_HEAD_DIM, S), (0, 3, 1, 2))
        kp_l.append(nk)
        vp_l.append(nv.reshape(B, S, DA_HEADS, 2 * DA_HEAD_DIM))
        rp_l.append(s_fin)

        p, nk, nv = _inproj(_rmsnorm(xs, g_mix_row, tm_s), w_in_bf, cos_s, sin_s, qg16, kg16, bd,
                            tm_s, None)
        yr_in, s_new = _retention(p, ret_gn_row, DB, L, L, state_ret[l].astype(F32))
        cache_kt = jnp.transpose(cache_da_k[l].reshape(DB, P, COL_W), (0, 2, 1))
        yd_in = _attention_sample(p, cache_kt, cache_da_v[l].reshape(DB, P * DA_HEADS, RET_DV),
                                  lam_rows, da_gn_row, DB, L, lam_init)
        x1, h2 = _merge(xs, yr_in, yd_in, p, w_ro, w_do, w_o, g_mlp_row, tm_s)
        xs = _mlp(x1, h2, w_u, w_d, tm_s, MLP_FF_TILE)
        ks_l.append(nk.reshape(DB, L, 2 * DA_HEADS, DA_HEAD_DIM))
        vs_l.append(nv.reshape(DB, L, DA_HEADS, 2 * DA_HEAD_DIM))
        rs_l.append(s_new)

    return (xp.reshape(B, S, D_MODEL), xs.reshape(DB, L, D_MODEL),
            jnp.stack(kp_l, 0), jnp.stack(vp_l, 0), jnp.stack(rp_l, 0),
            jnp.stack(ks_l, 0), jnp.stack(vs_l, 0), jnp.stack(rs_l, 0))
```

```python
import functools
import math

import numpy as np
import jax
import jax.numpy as jnp
from jax import lax
from jax.experimental import pallas as pl
from jax.experimental.pallas import tpu as pltpu

D_MODEL = 2048
CHUNK = 64
RET_HEADS = 8
RET_DK = 128
RET_DV = 128
DA_HEADS = 8
DA_HEAD_DIM = 64
D_FF = 4 * D_MODEL
ROPE_BASE = 10000.0
NORM_EPS = 1e-6
LOG2E = 1.4426950408889634

V7X_LANES = 128
V7X_MXU_DIM = 256
V7X_VMEM_LIMIT_BYTES = 56 * 1024 * 1024

COL_W = 1024
N_COL_TILES = 11
COL_QR, COL_KR, COL_VR, COL_ZR, COL_QD, COL_KD, COL_VD = 0, 1, 2, 3, 4, 5, 6
COL_GR, COL_GD = 7, 9

TOKEN_TILE = 512
INPROJ_TILE = 1024
RET_CHUNK_PROMPT = 256
MLP_FF_TILE = 1024
Q_SPLIT = 2

F32 = jnp.float32
BF16 = jnp.bfloat16

_LOG_G = [math.log1p(-(2.0 ** (-5.0 - h))) for h in range(RET_HEADS)]
_SLOPES = np.asarray([2.0 ** (-8.0 * (h + 1) / DA_HEADS) for h in range(DA_HEADS)], np.float32)
_SLOPES_LOG2 = (_SLOPES * np.float32(LOG2E)).astype(np.float32)
_Q_SCALE_LOG2 = (DA_HEAD_DIM ** -0.5) * LOG2E
_NEG = -1e30


def _cparams(n_grid):
    return pltpu.CompilerParams(
        dimension_semantics=("arbitrary",) * n_grid,
        vmem_limit_bytes=V7X_VMEM_LIMIT_BYTES)


def _pick_tile(n, cap):
    t = cap
    while n % t:
        t //= 2
    return t


def _rmsnorm_kernel(x_ref, g_ref, h_ref):
    x = x_ref[...]
    ms = jnp.mean(x * x, axis=-1, keepdims=True)
    h_ref[...] = (x * lax.rsqrt(ms + NORM_EPS) * g_ref[...]).astype(BF16)


def _rmsnorm(x2d, g_row, tm):
    T = x2d.shape[0]
    return pl.pallas_call(
        _rmsnorm_kernel, grid=(T // tm,),
        in_specs=[pl.BlockSpec((tm, D_MODEL), lambda i: (i, 0)),
                  pl.BlockSpec((1, D_MODEL), lambda i: (0, 0))],
        out_specs=pl.BlockSpec((tm, D_MODEL), lambda i: (i, 0)),
        out_shape=jax.ShapeDtypeStruct((T, D_MODEL), BF16),
        compiler_params=_cparams(1), name="rmsnorm",
    )(x2d, g_row)


def _inproj_kernel(h_ref, w_ref, ca_ref, sa_ref, sas_ref, cb_ref, sb_ref, sbs_ref,
                   qg_ref, kg_ref, bd_ref,
                   p_ref, nk_ref, nv_ref, *rest, emit_t, n_steps, n_tok):
    if emit_t:
        qt_ref, vt_ref, s_scr = rest
    else:
        (s_scr,) = rest
    t = pl.program_id(0)
    jb = (jnp.maximum(t, 1) - 1) // n_tok

    def matmul():
        s_scr[...] = jnp.dot(h_ref[...], w_ref[...], preferred_element_type=F32)

    def rope(scale):
        ca, sa, sas = ca_ref[0], sa_ref[0], sas_ref[0]
        cos = ca * cb_ref[...] - sa * sb_ref[...]
        sin = sas * cb_ref[...] + ca * sbs_ref[...]
        for hh in range(RET_HEADS):
            sl = slice(hh * RET_DK, (hh + 1) * RET_DK)
            a = s_scr[:, sl]
            o = a * cos + pltpu.roll(a, RET_DK // 2, axis=1) * sin
            if scale != 1.0:
                o = o * scale
            p_ref[:, sl] = o.astype(BF16)

    def plain():
        p_ref[...] = s_scr[...].astype(BF16)

    def store_feature_major(ref, val_t):
        for b in range(ref.shape[0]):
            ref[b] = val_t[:, b * TOKEN_TILE:(b + 1) * TOKEN_TILE]

    def head_norm(acc, gain_row):
        sq = (acc * acc).astype(BF16)
        parts = [
            jnp.dot(sq[:, c * V7X_MXU_DIM:(c + 1) * V7X_MXU_DIM], bd_ref[...],
                    preferred_element_type=F32)
            for c in range(COL_W // V7X_MXU_DIM)
        ]
        ms = jnp.concatenate(parts, axis=1)
        return acc * lax.rsqrt(ms + NORM_EPS) * gain_row

    def q_diff():
        n = head_norm(s_scr[...], qg_ref[...])
        p_ref[...] = n.astype(BF16)
        if emit_t:
            store_feature_major(qt_ref, (n * _Q_SCALE_LOG2).T.astype(BF16))

    def k_diff():
        n = head_norm(s_scr[...], kg_ref[...])
        if emit_t:
            nk_ref[0] = n.T
        else:
            nk_ref[...] = n
        p_ref[...] = n.astype(BF16)

    def v_diff():
        acc = s_scr[...]
        for h in range(DA_HEADS):
            nv_ref[pl.ds(h, acc.shape[0], stride=DA_HEADS), :] = acc[:, h * RET_DV:(h + 1) * RET_DV]
        p_ref[...] = acc.astype(BF16)
        if emit_t:
            store_feature_major(vt_ref, acc.T.astype(BF16))

    epilogues = {COL_QR: lambda: rope(1.0), COL_KR: lambda: rope(RET_DK ** -0.5),
                 COL_QD: q_diff, COL_KD: k_diff, COL_VD: v_diff}
    mid = (t > 0) & (t < n_steps - 1)

    @pl.when(t == 0)
    def _():
        matmul()

    for col, epilogue in epilogues.items():
        @pl.when(mid & (jb == col))
        def _(epilogue=epilogue):
            epilogue()
            matmul()

    @pl.when(mid & ((jb == COL_VR) | (jb == COL_ZR) | (jb >= COL_GR)))
    def _():
        plain()
        matmul()

    @pl.when(t == n_steps - 1)
    def _():
        plain()


def _rope_operands(base_pos, n_base, row_pos):
    half = RET_DK // 2
    inv = np.exp(-math.log(ROPE_BASE) * np.arange(half, dtype=np.float64) / half)
    sign = np.concatenate([-np.ones(half), np.ones(half)])

    def tables(pos):
        ang = np.asarray(pos, np.float64)[:, None] * inv[None, :]
        cos = np.concatenate([np.cos(ang)] * 2, axis=1)
        sin = np.concatenate([np.sin(ang)] * 2, axis=1)
        return cos, sin, sin * sign

    out_a = [jnp.asarray(x.reshape(n_base, 1, RET_DK), F32) for x in tables(base_pos)]
    out_b = [jnp.asarray(x, F32) for x in tables(row_pos)]
    return out_a + out_b


def _inproj(h2d, w_bf, rope_ops, qg16, kg16, bd, tm, seq_len):
    emit_t = seq_len is not None
    T = h2d.shape[0]
    n_tok = T // tm
    n_rope = rope_ops[0].shape[0]
    n_mm = n_tok * N_COL_TILES
    n_steps = n_mm + 1

    def mm_tile(t):
        tt = jnp.minimum(t, n_mm - 1)
        return tt // n_tok, tt % n_tok

    def ep_tile(t):
        tt = jnp.maximum(t, 1) - 1
        return tt // n_tok, tt % n_tok

    def side_tile(t, col):
        j, i = ep_tile(t)
        return jnp.where(j < col, 0, jnp.where(j > col, n_tok - 1, i))

    base_spec = pl.BlockSpec((1, 1, RET_DK), lambda t: (ep_tile(t)[1] % n_rope, 0, 0))
    row_spec = pl.BlockSpec((tm, RET_DK), lambda t: (0, 0))
    in_specs = [
        pl.BlockSpec((tm, D_MODEL), lambda t: (mm_tile(t)[1], 0)),
        pl.BlockSpec((D_MODEL, COL_W), lambda t: (0, mm_tile(t)[0])),
        base_spec, base_spec, base_spec, row_spec, row_spec, row_spec,
        pl.BlockSpec((1, COL_W), lambda t: (0, 0)),
        pl.BlockSpec((1, COL_W), lambda t: (0, 0)),
        pl.BlockSpec((V7X_MXU_DIM, V7X_MXU_DIM), lambda t: (0, 0)),
    ]
    if emit_t:
        spb = seq_len // tm
        nk_shape = jax.ShapeDtypeStruct((T // seq_len, COL_W, seq_len), F32)
        nk_spec = pl.BlockSpec((1, COL_W, tm), lambda t: (side_tile(t, COL_KD) // spb, 0,
                                                          side_tile(t, COL_KD) % spb))
    else:
        nk_shape = jax.ShapeDtypeStruct((T, COL_W), F32)
        nk_spec = pl.BlockSpec((tm, COL_W), lambda t: (side_tile(t, COL_KD), 0))
    out_shape = [
        jax.ShapeDtypeStruct((T, N_COL_TILES * COL_W), BF16),
        nk_shape,
        jax.ShapeDtypeStruct((T * DA_HEADS, RET_DV), F32),
    ]
    out_specs = [
        pl.BlockSpec((tm, COL_W), lambda t: (ep_tile(t)[1], ep_tile(t)[0])),
        nk_spec,
        pl.BlockSpec((tm * DA_HEADS, RET_DV), lambda t: (side_tile(t, COL_VD), 0)),
    ]
    if emit_t:
        ab = tm // TOKEN_TILE
        out_shape += [jax.ShapeDtypeStruct((T // TOKEN_TILE, COL_W, TOKEN_TILE), BF16)] * 2
        out_specs += [
            pl.BlockSpec((ab, COL_W, TOKEN_TILE), lambda t: (side_tile(t, COL_QD), 0, 0)),
            pl.BlockSpec((ab, COL_W, TOKEN_TILE), lambda t: (side_tile(t, COL_VD), 0, 0))]
    return pl.pallas_call(
        functools.partial(_inproj_kernel, emit_t=emit_t, n_steps=n_steps, n_tok=n_tok),
        grid=(n_steps,), in_specs=in_specs, out_specs=out_specs, out_shape=out_shape,
        scratch_shapes=[pltpu.VMEM((tm, COL_W), F32)],
        compiler_params=_cparams(1), name="inproj_t" if emit_t else "inproj",
    )(h2d, w_bf, *rope_ops, qg16, kg16, bd)


def _ret_kernel(q_ref, k_ref, v_ref, z_ref, gn_ref, *rest, C, has_init):
    if has_init:
        s0_ref, y_ref, s_ref, dm_scr, rd_scr, kd_scr = rest
    else:
        y_ref, s_ref, dm_scr, rd_scr, kd_scr = rest
    b = pl.program_id(0)
    c = pl.program_id(1)

    @pl.when((b == 0) & (c == 0))
    def _():
        li = lax.broadcasted_iota(jnp.int32, (C, C), 0)
        mi = lax.broadcasted_iota(jnp.int32, (C, C), 1)
        causal = li >= mi
        diff = jnp.where(causal, (li - mi).astype(F32), 0.0)
        row = lax.broadcasted_iota(jnp.int32, (C, RET_DK), 0).astype(F32)
        for h in range(RET_HEADS):
            dm_scr[h] = jnp.where(causal, jnp.exp(diff * _LOG_G[h]), 0.0)
            rd_scr[h] = jnp.exp((row + 1.0) * _LOG_G[h])
            kd_scr[h] = jnp.exp((C - 1.0 - row) * _LOG_G[h])

    @pl.when(c == 0)
    def _():
        if has_init:
            s_ref[...] = s0_ref[...]
        else:
            s_ref[...] = jnp.zeros(s_ref.shape, F32)

    for h in range(RET_HEADS):
        sl = slice(h * RET_DK, (h + 1) * RET_DK)
        q = q_ref[:, sl]
        k = k_ref[:, sl]
        v = v_ref[:, sl]
        state = s_ref[0, h]
        sc = lax.dot_general(q, k, (((1,), (1,)), ((), ())), preferred_element_type=F32)
        sc = sc * dm_scr[h]
        intra = jnp.dot(sc.astype(BF16), v, preferred_element_type=F32)
        inter = jnp.dot(q, state.astype(BF16), preferred_element_type=F32) * rd_scr[h]
        o = intra + inter
        kw = (k.astype(F32) * kd_scr[h]).astype(BF16)
        upd = lax.dot_general(kw, v, (((0,), (0,)), ((), ())), preferred_element_type=F32)
        s_ref[0, h] = state * math.exp(C * _LOG_G[h]) + upd
        ms = jnp.mean(o * o, axis=-1, keepdims=True)
        z = z_ref[:, sl].astype(F32)
        y = o * lax.rsqrt(ms + NORM_EPS) * gn_ref[:, sl] * (z * jax.nn.sigmoid(z))
        y_ref[:, sl] = y.astype(BF16)


def _retention(p, ret_gn_row, n_seq, seq_len, C, init_state):
    n_chunks = seq_len // C
    has_init = init_state is not None

    def col(t):
        return pl.BlockSpec((C, COL_W), lambda b, c: (b * n_chunks + c, t))

    in_specs = [col(COL_QR), col(COL_KR), col(COL_VR), col(COL_ZR),
                pl.BlockSpec((1, COL_W), lambda b, c: (0, 0))]
    args = [p, p, p, p, ret_gn_row]
    state_spec = pl.BlockSpec((1, RET_HEADS, RET_DK, RET_DV), lambda b, c: (b, 0, 0, 0))
    if has_init:
        in_specs.append(state_spec)
        args.append(init_state)
    return pl.pallas_call(
        functools.partial(_ret_kernel, C=C, has_init=has_init),
        grid=(n_seq, n_chunks), in_specs=in_specs,
        out_specs=[pl.BlockSpec((C, COL_W), lambda b, c: (b * n_chunks + c, 0)), state_spec],
        out_shape=[jax.ShapeDtypeStruct((n_seq * seq_len, COL_W), BF16),
                   jax.ShapeDtypeStruct((n_seq, RET_HEADS, RET_DK, RET_DV), F32)],
        scratch_shapes=[pltpu.VMEM((RET_HEADS, C, C), F32),
                        pltpu.VMEM((RET_HEADS, C, RET_DK), F32),
                        pltpu.VMEM((RET_HEADS, C, RET_DK), F32)],
        compiler_params=_cparams(2), name="retention_init" if has_init else "retention",
    )(*args)


def _lambda_value(lq1, lk1, lq2, lk2, lam_init):
    a = jnp.sum(lq1[...] * lk1[...], axis=-1, keepdims=True)
    b = jnp.sum(lq2[...] * lk2[...], axis=-1, keepdims=True)
    return jnp.exp(a) - jnp.exp(b) + lam_init


def _bf16_split3(x):
    out, r = [], np.float64(x)
    for _ in range(3):
        c = np.float64(np.asarray(r, np.float32).astype(jnp.bfloat16).astype(np.float32))
        out.append(float(c))
        r = r - c
    return out


def _alibi_operands(blk):
    c = _bf16_split3(LOG2E)
    pos = np.arange(blk)
    ek = np.zeros((blk, 2 * DA_HEAD_DIM), np.float32)
    bq = np.zeros((2 * DA_HEAD_DIM, blk), np.float32)
    for t in range(3):
        ek[:, t] = pos >> 4
        ek[:, 3 + t] = pos & 15
        ek[:, 6 + t] = -16.0 * c[t]
        ek[:, 9 + t] = -c[t]
        bq[t, :] = 16.0 * c[t]
        bq[3 + t, :] = c[t]
        bq[6 + t, :] = pos >> 4
        bq[9 + t, :] = pos & 15
    return jnp.asarray(ek, BF16), jnp.asarray(bq, BF16)


def _attn_kernel(slog2_ref, spow2_ref, qt1_ref, qt2_ref, k1_ref, k2_ref, vt_ref, ek_ref, bq_ref,
                 lq1, lk1, lq2, lk2, gn_ref, o_ref,
                 qx_scr, m_scr, l_scr, acc_scr, s_scr, mb_scr, *, blk, lam_init):
    j = pl.program_id(1)
    qi = pl.program_id(2)
    half = DA_HEAD_DIM
    hd2 = 2 * DA_HEAD_DIM
    k_refs = (k1_ref, k2_ref)

    zeros = jnp.zeros((half, blk), BF16)
    for hh in range(2):
        bias_rows = (bq_ref[...].astype(F32) * spow2_ref[2 * j + hh]).astype(BF16)
        for c, qref in enumerate((qt1_ref, qt2_ref)):
            idx = 2 * c + hh
            if hh == 0:
                qx_scr[idx, :half, :] = qref[0, :half, :]
                qx_scr[idx, half:hd2, :] = zeros
            else:
                qx_scr[idx, :half, :] = zeros
                qx_scr[idx, half:hd2, :] = qref[0, half:, :]
            qx_scr[idx, hd2:, :] = bias_rows
    m_scr[...] = jnp.full(m_scr.shape, _NEG, F32)
    l_scr[...] = jnp.zeros(l_scr.shape, F32)
    acc_scr[...] = jnp.zeros(acc_scr.shape, F32)

    def score_diag(par):
        kk = lax.broadcasted_iota(jnp.int32, (blk, blk), 0)
        qq = lax.broadcasted_iota(jnp.int32, (blk, blk), 1)
        allowed = (kk // CHUNK) <= (qq // CHUNK)
        dist = jnp.abs(kk - qq).astype(F32)
        row0 = pl.multiple_of(qi * blk, blk)
        for c in range(2):
            kc = k_refs[c][pl.ds(row0, blk), :]
            for hh in range(2):
                idx = 2 * c + hh
                s = jnp.dot(kc, qx_scr[idx, :hd2, :], preferred_element_type=F32)
                s = jnp.where(allowed, s - slog2_ref[2 * j + hh] * dist, _NEG)
                s_scr[par, idx] = s
                mb_scr[par, idx] = jnp.max(s, axis=0, keepdims=True)

    qw = blk // Q_SPLIT
    tiles = [(idx, pl.ds(h * qw, qw)) for idx in range(4) for h in range(Q_SPLIT)]

    def score_tile(ki, par, idx, qs):
        row0 = pl.multiple_of(ki * blk, blk)
        kx = jnp.concatenate([k_refs[idx // 2][pl.ds(row0, blk), :], ek_ref[...]], axis=1)
        s = jnp.dot(kx, qx_scr[idx, :, qs], preferred_element_type=F32)
        s_scr[par, idx, :, qs] = s
        mb_scr[par, idx, :, qs] = jnp.max(s, axis=0, keepdims=True)

    def consume_tile(kb, par, idx, qs):
        hh = idx % 2
        shift = slog2_ref[2 * j + hh] * ((qi - kb) * blk).astype(F32)
        m_old = m_scr[idx, :, qs]
        m_new = jnp.maximum(m_old, mb_scr[par, idx, :, qs] - shift)
        alpha = jnp.exp2(m_old - m_new)
        p = jnp.exp2(s_scr[par, idx, :, qs] - (m_new + shift))
        l_scr[idx, :, qs] = alpha * l_scr[idx, :, qs] + jnp.sum(p, axis=0, keepdims=True)
        vt_h = vt_ref[kb, hh * hd2:(hh + 1) * hd2, :]
        acc_scr[idx, :, qs] = alpha * acc_scr[idx, :, qs] + jnp.dot(
            vt_h, p.astype(BF16), preferred_element_type=F32)
        m_scr[idx, :, qs] = m_new

    def consume(kb, par):
        for idx, qs in tiles:
            consume_tile(kb, par, idx, qs)

    def stage(ki, par, kb):
        for idx, qs in tiles:
            score_tile(ki, par, idx, qs)
            consume_tile(kb, 1 - par, idx, qs)

    score_diag(0)

    def pair_body(u, carry):
        t0 = 2 * u
        stage(t0, 1, jnp.where(t0 == 0, qi, t0 - 1))
        stage(t0 + 1, 0, t0)
        return carry

    lax.fori_loop(0, qi // 2, pair_body, 0)

    @pl.when(qi % 2 == 1)
    def _():
        t0 = qi - 1
        stage(t0, 1, jnp.where(t0 == 0, qi, t0 - 1))
        consume(t0, 1)

    @pl.when(qi % 2 == 0)
    def _():
        consume(jnp.where(qi == 0, qi, qi - 1), 0)

    lam = _lambda_value(lq1, lk1, lq2, lk2, lam_init)
    for hh in range(2):
        o = acc_scr[hh] / l_scr[hh] - lam * (acc_scr[2 + hh] / l_scr[2 + hh])
        ms = jnp.mean(o * o, axis=0, keepdims=True)
        on = (o * lax.rsqrt(ms + NORM_EPS)).T
        sl = slice(hh * hd2, (hh + 1) * hd2)
        o_ref[:, sl] = (on * gn_ref[:, sl] * (1.0 - lam_init)).astype(BF16)


def _attention_prompt(p, qt, vt, lam_rows, da_gn_row, n_seq, seq_len, blk, lam_init):
    nb = seq_len // blk
    n_hp = DA_HEADS // 2
    hd2 = 2 * DA_HEAD_DIM
    kcol0 = COL_KD * COL_W // hd2
    ek, bq = _alibi_operands(blk)

    smem = pl.BlockSpec(memory_space=pltpu.SMEM)
    in_specs = [
        smem, smem,
        pl.BlockSpec((1, hd2, blk), lambda b, j, q: (b * nb + q, j, 0)),
        pl.BlockSpec((1, hd2, blk), lambda b, j, q: (b * nb + q, n_hp + j, 0)),
        pl.BlockSpec((seq_len, hd2), lambda b, j, q: (b, kcol0 + j)),
        pl.BlockSpec((seq_len, hd2), lambda b, j, q: (b, kcol0 + n_hp + j)),
        pl.BlockSpec((nb, 2 * hd2, blk), lambda b, j, q: (b, j, 0)),
        pl.BlockSpec((blk, hd2), lambda b, j, q: (0, 0)),
        pl.BlockSpec((hd2, blk), lambda b, j, q: (0, 0)),
    ] + [pl.BlockSpec((1, DA_HEAD_DIM), lambda b, j, q: (0, 0))] * 4 + [
        pl.BlockSpec((1, 2 * hd2), lambda b, j, q: (0, j)),
    ]
    return pl.pallas_call(
        functools.partial(_attn_kernel, blk=blk, lam_init=lam_init),
        grid=(n_seq, n_hp, nb), in_specs=in_specs,
        out_specs=pl.BlockSpec((blk, 2 * hd2), lambda b, j, q: (b * nb + q, j)),
        out_shape=jax.ShapeDtypeStruct((n_seq * seq_len, COL_W), BF16),
        scratch_shapes=[pltpu.VMEM((4, 2 * hd2, blk), BF16),
                        pltpu.VMEM((4, 1, blk), F32),
                        pltpu.VMEM((4, 1, blk), F32),
                        pltpu.VMEM((4, hd2, blk), F32),
                        pltpu.VMEM((2, 4, blk, blk), F32),
                        pltpu.VMEM((2, 4, 1, blk), F32)],
        compiler_params=_cparams(3), name="diff_attn_prompt",
    )(jnp.asarray(_SLOPES_LOG2), jnp.asarray(_SLOPES), qt, qt, p, p, vt, ek, bq,
      *lam_rows, da_gn_row)


def _sattn_kernel(slopes_ref, q_ref, kn_ref, vn_ref, kc_ref, vc_ref,
                  lq1, lk1, lq2, lk2, gn_ref, o_ref, *, L, P, lam_init):
    lam = _lambda_value(lq1, lk1, lq2, lk2, lam_init)
    hd2 = 2 * DA_HEAD_DIM
    lane = lax.broadcasted_iota(jnp.int32, (L, hd2), 1)
    qpos_c = P + lax.broadcasted_iota(jnp.int32, (L, P), 0)
    kpos_c = lax.broadcasted_iota(jnp.int32, (L, P), 1)
    qpos_n = P + lax.broadcasted_iota(jnp.int32, (L, L), 0)
    kpos_n = P + lax.broadcasted_iota(jnp.int32, (L, L), 1)
    dist_c = jnp.abs(qpos_c - kpos_c).astype(F32)
    dist_n = jnp.abs(qpos_n - kpos_n).astype(F32)
    ok_c = (kpos_c // CHUNK) <= (qpos_c // CHUNK)
    ok_n = (kpos_n // CHUNK) <= (qpos_n // CHUNK)
    nt = (((1,), (1,)), ((), ()))
    for pr in range(DA_HEADS // 2):
        for hh in range(2):
            h = 2 * pr + hh
            slope = slopes_ref[h]
            probs = []
            for c in range(2):
                sl = slice((c * (DA_HEADS // 2) + pr) * hd2, (c * (DA_HEADS // 2) + pr + 1) * hd2)
                q128 = q_ref[:, sl].astype(F32) * _Q_SCALE_LOG2
                keep = (lane < DA_HEAD_DIM) if hh == 0 else (lane >= DA_HEAD_DIM)
                qm = jnp.where(keep, q128, 0.0).astype(BF16)
                kc_t = kc_ref[0, sl, :].astype(BF16)
                kn = kn_ref[:, sl]
                s_c = jnp.dot(qm, kc_t, preferred_element_type=F32)
                s_n = lax.dot_general(qm, kn, nt, preferred_element_type=F32)
                s_c = jnp.where(ok_c, s_c - slope * dist_c, _NEG)
                s_n = jnp.where(ok_n, s_n - slope * dist_n, _NEG)
                m = jnp.maximum(jnp.max(s_c, axis=-1, keepdims=True),
                                jnp.max(s_n, axis=-1, keepdims=True))
                p_c = jnp.exp2(s_c - m)
                p_n = jnp.exp2(s_n - m)
                inv = 1.0 / (jnp.sum(p_c, axis=-1, keepdims=True)
                             + jnp.sum(p_n, axis=-1, keepdims=True))
                probs.append((p_c * inv, p_n * inv))
            a_c = (probs[0][0] - lam * probs[1][0]).astype(BF16)
            a_n = (probs[0][1] - lam * probs[1][1]).astype(BF16)
            vsl = slice(h * hd2, (h + 1) * hd2)
            vc = vc_ref[0, pl.ds(h, P, stride=DA_HEADS), :].astype(BF16)
            o = (jnp.dot(a_c, vc, preferred_element_type=F32)
                 + jnp.dot(a_n, vn_ref[:, vsl], preferred_element_type=F32))
            ms = jnp.mean(o * o, axis=-1, keepdims=True)
            on = o * lax.rsqrt(ms + NORM_EPS) * gn_ref[:, vsl] * (1.0 - lam_init)
            o_ref[:, vsl] = on.astype(BF16)


def _attention_sample(p, cache_kt, cache_v, lam_rows, da_gn_row, n_seq, L, lam_init):
    P = cache_kt.shape[2]
    smem = pl.BlockSpec(memory_space=pltpu.SMEM)
    in_specs = [
        smem,
        pl.BlockSpec((L, COL_W), lambda s: (s, COL_QD)),
        pl.BlockSpec((L, COL_W), lambda s: (s, COL_KD)),
        pl.BlockSpec((L, COL_W), lambda s: (s, COL_VD)),
        pl.BlockSpec((1, COL_W, P), lambda s: (s, 0, 0)),
        pl.BlockSpec((1, P * DA_HEADS, RET_DV), lambda s: (s, 0, 0)),
    ] + [pl.BlockSpec((1, DA_HEAD_DIM), lambda s: (0, 0))] * 4 + [
        pl.BlockSpec((1, COL_W), lambda s: (0, 0)),
    ]
    return pl.pallas_call(
        functools.partial(_sattn_kernel, L=L, P=P, lam_init=lam_init),
        grid=(n_seq,), in_specs=in_specs,
        out_specs=pl.BlockSpec((L, COL_W), lambda s: (s, 0)),
        out_shape=jax.ShapeDtypeStruct((n_seq * L, COL_W), BF16),
        compiler_params=_cparams(1), name="diff_attn_sample",
    )(jnp.asarray(_SLOPES_LOG2), p, p, p, cache_kt, cache_v, *lam_rows, da_gn_row)


def _merge_kernel(x_ref, yr_ref, yd_ref, gr0, gr1, gd0, gd1, wr_ref, wd_ref, wo_ref, gm_ref,
                  x1_ref, h2_ref):
    yr = jnp.dot(yr_ref[...], wr_ref[...], preferred_element_type=F32)
    yd = jnp.dot(yd_ref[...], wd_ref[...], preferred_element_type=F32)
    gr = jnp.concatenate([gr0[...], gr1[...]], axis=1).astype(F32)
    gd = jnp.concatenate([gd0[...], gd1[...]], axis=1).astype(F32)
    m = jax.nn.sigmoid(gr) * yr + jax.nn.sigmoid(gd) * yd
    x1 = x_ref[...] + jnp.dot(m.astype(BF16), wo_ref[...], preferred_element_type=F32)
    x1_ref[...] = x1
    ms = jnp.mean(x1 * x1, axis=-1, keepdims=True)
    h2_ref[...] = (x1 * lax.rsqrt(ms + NORM_EPS) * gm_ref[...]).astype(BF16)


def _merge(x2d, yr_in, yd_in, p, w_ret_o, w_da_o, w_out, g_mlp_row, tm):
    T = x2d.shape[0]

    def whole(shape):
        return pl.BlockSpec(shape, lambda i: (0, 0), pipeline_mode=pl.Buffered(1))

    def pcol(t):
        return pl.BlockSpec((tm, COL_W), lambda i: (i, t))

    in_specs = [
        pl.BlockSpec((tm, D_MODEL), lambda i: (i, 0)),
        pl.BlockSpec((tm, COL_W), lambda i: (i, 0)),
        pl.BlockSpec((tm, COL_W), lambda i: (i, 0)),
        pcol(COL_GR), pcol(COL_GR + 1), pcol(COL_GD), pcol(COL_GD + 1),
        whole((COL_W, D_MODEL)), whole((COL_W, D_MODEL)), whole((D_MODEL, D_MODEL)),
        pl.BlockSpec((1, D_MODEL), lambda i: (0, 0)),
    ]
    return pl.pallas_call(
        _merge_kernel, grid=(T // tm,), in_specs=in_specs,
        out_specs=[pl.BlockSpec((tm, D_MODEL), lambda i: (i, 0)),
                   pl.BlockSpec((tm, D_MODEL), lambda i: (i, 0))],
        out_shape=[jax.ShapeDtypeStruct((T, D_MODEL), F32),
                   jax.ShapeDtypeStruct((T, D_MODEL), BF16)],
        compiler_params=_cparams(1), name="merge",
    )(x2d, yr_in, yd_in, p, p, p, p, w_ret_o, w_da_o, w_out, g_mlp_row)


def _mlp_kernel(x1_ref, h2_ref, wu_ref, wd_ref, y_ref, r_scr, *, n_steps, nf):
    t = pl.program_id(0)
    fb = (jnp.maximum(t, 1) - 1) % nf

    def up(par):
        u = jnp.dot(h2_ref[...], wu_ref[...], preferred_element_type=F32)
        r_scr[par] = jnp.square(jnp.maximum(u, 0.0)).astype(BF16)

    def down(par, first_chunk):
        contrib = jnp.dot(r_scr[par], wd_ref[...], preferred_element_type=F32)
        if first_chunk:
            y_ref[...] = x1_ref[...] + contrib
        else:
            y_ref[...] += contrib

    mid = (t > 0) & (t < n_steps - 1)

    @pl.when(t == 0)
    def _():
        up(0)

    @pl.when(mid & (t % 2 == 0))
    def _():
        down(1, False)
        up(0)

    @pl.when(mid & (t % 2 == 1) & (fb == 0))
    def _():
        down(0, True)
        up(1)

    @pl.when(mid & (t % 2 == 1) & (fb != 0))
    def _():
        down(0, False)
        up(1)

    @pl.when(t == n_steps - 1)
    def _():
        down((n_steps - 2) % 2, False)


def _mlp(x1, h2, w_up, w_down, tm, tf):
    T = x1.shape[0]
    nf = D_FF // tf
    assert nf % 2 == 0 and nf >= 2
    n_up = (T // tm) * nf
    n_steps = n_up + 1

    def up_idx(t):
        return jnp.minimum(t, n_up - 1)

    def down_idx(t):
        return jnp.maximum(t, 1) - 1

    return pl.pallas_call(
        functools.partial(_mlp_kernel, n_steps=n_steps, nf=nf), grid=(n_steps,),
        in_specs=[pl.BlockSpec((tm, D_MODEL), lambda t: (down_idx(t) // nf, 0)),
                  pl.BlockSpec((tm, D_MODEL), lambda t: (up_idx(t) // nf, 0)),
                  pl.BlockSpec((D_MODEL, tf), lambda t: (0, up_idx(t) % nf)),
                  pl.BlockSpec((tf, D_MODEL), lambda t: (down_idx(t) % nf, 0))],
        out_specs=pl.BlockSpec((tm, D_MODEL), lambda t: (down_idx(t) // nf, 0)),
        out_shape=jax.ShapeDtypeStruct((T, D_MODEL), F32),
        scratch_shapes=[pltpu.VMEM((2, tm, tf), BF16)],
        compiler_params=_cparams(1), name="mlp",
    )(x1, h2, w_up, w_down)


def _block_diag_mean():
    r = np.arange(V7X_MXU_DIM) // DA_HEAD_DIM
    return jnp.asarray((r[:, None] == r[None, :]).astype(np.float32) / DA_HEAD_DIM, BF16)


def kernel(x_prompt, x_sample, cache_da_k, cache_da_v, state_ret, g_mix, w_in, ret_gn, qn_g, kn_g,
           lambda_q1, lambda_k1, lambda_q2, lambda_k2, da_gn, w_ret_o, w_da_o, w_out, g_mlp,
           w_up, w_down):
    B, S, _ = x_prompt.shape
    DB, L, _ = x_sample.shape
    P = cache_da_k.shape[2]
    depth = w_in.shape[0]
    assert S % TOKEN_TILE == 0 and S % RET_CHUNK_PROMPT == 0 and L % 16 == 0

    tm_p = TOKEN_TILE
    tm_s = _pick_tile(DB * L, TOKEN_TILE)
    tm_ip = _pick_tile(S, INPROJ_TILE)
    tm_is = _pick_tile(DB * L, INPROJ_TILE)
    assert tm_ip % TOKEN_TILE == 0 and (tm_is % L == 0 or L % tm_is == 0)
    rope_p = _rope_operands(np.arange(S // tm_ip) * tm_ip, S // tm_ip, np.arange(tm_ip))
    if tm_is >= L:
        rope_s = _rope_operands([0], 1, P + np.arange(tm_is) % L)
    else:
        rope_s = _rope_operands(P + np.arange(L // tm_is) * tm_is, L // tm_is, np.arange(tm_is))
    bd = _block_diag_mean()
    n_heads_qk = COL_W // DA_HEAD_DIM

    xp = x_prompt.reshape(B * S, D_MODEL)
    xs = x_sample.reshape(DB * L, D_MODEL)
    kp_l, vp_l, rp_l, ks_l, vs_l, rs_l = [], [], [], [], [], []
    for l in range(depth):
        lam_init = 0.8 - 0.6 * math.exp(-0.3 * l)
        w_in_bf = w_in[l].astype(BF16)
        w_ro, w_do, w_o = w_ret_o[l].astype(BF16), w_da_o[l].astype(BF16), w_out[l].astype(BF16)
        w_u, w_d = w_up[l].astype(BF16), w_down[l].astype(BF16)
        g_mix_row = g_mix[l].reshape(1, D_MODEL)
        g_mlp_row = g_mlp[l].reshape(1, D_MODEL)
        qg16 = jnp.tile(qn_g[l], n_heads_qk).reshape(1, COL_W)
        kg16 = jnp.tile(kn_g[l], n_heads_qk).reshape(1, COL_W)
        ret_gn_row = ret_gn[l].reshape(1, COL_W)
        da_gn_row = da_gn[l].reshape(1, COL_W)
        lam_rows = [a[l].astype(F32).reshape(1, DA_HEAD_DIM)
                    for a in (lambda_q1, lambda_k1, lambda_q2, lambda_k2)]

        p, nkt, nv, qt, vt = _inproj(_rmsnorm(xp, g_mix_row, tm_ip), w_in_bf, rope_p,
                                     qg16, kg16, bd, tm_ip, S)
        yr_in, s_fin = _retention(p, ret_gn_row, B, S, RET_CHUNK_PROMPT, None)
        yd_in = _attention_prompt(p, qt, vt, lam_rows, da_gn_row, B, S, TOKEN_TILE, lam_init)
        x1, h2 = _merge(xp, yr_in, yd_in, p, w_ro, w_do, w_o, g_mlp_row, tm_p)
        xp = _mlp(x1, h2, w_u, w_d, tm_p, MLP_FF_TILE)
        nk = jnp.transpose(nkt.reshape(B, 2 * DA_HEADS, DA_HEAD_DIM, S), (0, 3, 1, 2))
        kp_l.append(nk)
        vp_l.append(nv.reshape(B, S, DA_HEADS, 2 * DA_HEAD_DIM))
        rp_l.append(s_fin)

        p, nk, nv = _inproj(_rmsnorm(xs, g_mix_row, tm_is), w_in_bf, rope_s, qg16, kg16, bd,
                            tm_is, None)
        yr_in, s_new = _retention(p, ret_gn_row, DB, L, L, state_ret[l].astype(F32))
        cache_kt = jnp.transpose(cache_da_k[l].reshape(DB, P, COL_W), (0, 2, 1))
        yd_in = _attention_sample(p, cache_kt, cache_da_v[l].reshape(DB, P * DA_HEADS, RET_DV),
                                  lam_rows, da_gn_row, DB, L, lam_init)
        x1, h2 = _merge(xs, yr_in, yd_in, p, w_ro, w_do, w_o, g_mlp_row, tm_s)
        xs = _mlp(x1, h2, w_u, w_d, tm_s, MLP_FF_TILE)
        ks_l.append(nk.reshape(DB, L, 2 * DA_HEADS, DA_HEAD_DIM))
        vs_l.append(nv.reshape(DB, L, DA_HEADS, 2 * DA_HEAD_DIM))
        rs_l.append(s_new)

    return (xp.reshape(B, S, D_MODEL), xs.reshape(DB, L, D_MODEL),
            jnp.stack(kp_l, 0), jnp.stack(vp_l, 0), jnp.stack(rp_l, 0),
            jnp.stack(ks_l, 0), jnp.stack(vs_l, 0), jnp.stack(rs_l, 0))
```

```python
import functools
import math

import numpy as np
import jax
import jax.numpy as jnp
from jax import lax
from jax.experimental import pallas as pl
from jax.experimental.pallas import tpu as pltpu

D_MODEL = 2048
CHUNK = 64
RET_HEADS = 8
RET_DK = 128
RET_DV = 128
DA_HEADS = 8
DA_HEAD_DIM = 64
D_FF = 4 * D_MODEL
ROPE_BASE = 10000.0
NORM_EPS = 1e-6
LOG2E = 1.4426950408889634

V7X_MXU_DIM = 256
V7X_VMEM_LIMIT_BYTES = 56 * 1024 * 1024

COL_W = 1024
N_COL_TILES = 11
COL_QR, COL_KR, COL_VR, COL_ZR, COL_QD, COL_KD, COL_VD = 0, 1, 2, 3, 4, 5, 6
COL_GR, COL_GD = 7, 9

TOKEN_TILE = 512
INPROJ_TILE = 1024
RET_CHUNK_PROMPT = 256
MLP_FF_TILE = 1024
Q_SPLIT = 2

F32 = jnp.float32
BF16 = jnp.bfloat16

_LOG_G = [math.log1p(-(2.0 ** (-5.0 - h))) for h in range(RET_HEADS)]
_SLOPES = np.asarray([2.0 ** (-8.0 * (h + 1) / DA_HEADS) for h in range(DA_HEADS)], np.float32)
_SLOPES_LOG2 = (_SLOPES * np.float32(LOG2E)).astype(np.float32)
_Q_SCALE_LOG2 = (DA_HEAD_DIM ** -0.5) * LOG2E
_NEG = -1e30


def _cparams(n_grid):
    return pltpu.CompilerParams(
        dimension_semantics=("arbitrary",) * n_grid,
        vmem_limit_bytes=V7X_VMEM_LIMIT_BYTES)


def _pick_tile(n, cap):
    t = cap
    while n % t:
        t //= 2
    return t


def _rmsnorm_kernel(x_ref, g_ref, h_ref):
    x = x_ref[...]
    ms = jnp.mean(x * x, axis=-1, keepdims=True)
    h_ref[...] = (x * lax.rsqrt(ms + NORM_EPS) * g_ref[...]).astype(BF16)


def _rmsnorm(x2d, g_row, tm):
    T = x2d.shape[0]
    return pl.pallas_call(
        _rmsnorm_kernel, grid=(T // tm,),
        in_specs=[pl.BlockSpec((tm, D_MODEL), lambda i: (i, 0)),
                  pl.BlockSpec((1, D_MODEL), lambda i: (0, 0))],
        out_specs=pl.BlockSpec((tm, D_MODEL), lambda i: (i, 0)),
        out_shape=jax.ShapeDtypeStruct((T, D_MODEL), BF16),
        compiler_params=_cparams(1), name="rmsnorm",
    )(x2d, g_row)


def _inproj_kernel(h_ref, w_ref, ca_ref, sa_ref, sas_ref, cb_ref, sb_ref, sbs_ref,
                   qg_ref, kg_ref, bd_ref,
                   p_ref, nk_ref, nv_ref, *rest, emit_t, n_steps, n_tok):
    if emit_t:
        qt_ref, vt_ref, s_scr = rest
    else:
        (s_scr,) = rest
    t = pl.program_id(0)
    jb = (jnp.maximum(t, 1) - 1) // n_tok

    def matmul():
        s_scr[...] = jnp.dot(h_ref[...], w_ref[...], preferred_element_type=F32)

    def rope(scale):
        ca, sa, sas = ca_ref[0], sa_ref[0], sas_ref[0]
        cos = ca * cb_ref[...] - sa * sb_ref[...]
        sin = sas * cb_ref[...] + ca * sbs_ref[...]
        for hh in range(RET_HEADS):
            sl = slice(hh * RET_DK, (hh + 1) * RET_DK)
            a = s_scr[:, sl]
            o = a * cos + pltpu.roll(a, RET_DK // 2, axis=1) * sin
            if scale != 1.0:
                o = o * scale
            p_ref[:, sl] = o.astype(BF16)

    def plain():
        p_ref[...] = s_scr[...].astype(BF16)

    def store_feature_major(ref, val_t):
        for b in range(ref.shape[0]):
            ref[b] = val_t[:, b * TOKEN_TILE:(b + 1) * TOKEN_TILE]

    def head_norm(acc, gain_row):
        sq = (acc * acc).astype(BF16)
        parts = [
            jnp.dot(sq[:, c * V7X_MXU_DIM:(c + 1) * V7X_MXU_DIM], bd_ref[...],
                    preferred_element_type=F32)
            for c in range(COL_W // V7X_MXU_DIM)
        ]
        ms = jnp.concatenate(parts, axis=1)
        return acc * lax.rsqrt(ms + NORM_EPS) * gain_row

    def q_diff():
        n = head_norm(s_scr[...], qg_ref[...])
        p_ref[...] = n.astype(BF16)
        if emit_t:
            store_feature_major(qt_ref, (n * _Q_SCALE_LOG2).T.astype(BF16))

    def k_diff():
        n = head_norm(s_scr[...], kg_ref[...])
        if emit_t:
            nk_ref[0] = n.T
        else:
            nk_ref[...] = n
        p_ref[...] = n.astype(BF16)

    def v_diff():
        acc = s_scr[...]
        for h in range(DA_HEADS):
            nv_ref[pl.ds(h, acc.shape[0], stride=DA_HEADS), :] = acc[:, h * RET_DV:(h + 1) * RET_DV]
        p_ref[...] = acc.astype(BF16)
        if emit_t:
            store_feature_major(vt_ref, acc.T.astype(BF16))

    epilogues = {COL_QR: lambda: rope(1.0), COL_KR: lambda: rope(RET_DK ** -0.5),
                 COL_QD: q_diff, COL_KD: k_diff, COL_VD: v_diff}
    mid = (t > 0) & (t < n_steps - 1)

    @pl.when(t == 0)
    def _():
        matmul()

    for col, epilogue in epilogues.items():
        @pl.when(mid & (jb == col))
        def _(epilogue=epilogue):
            epilogue()
            matmul()

    @pl.when(mid & ((jb == COL_VR) | (jb == COL_ZR) | (jb >= COL_GR)))
    def _():
        plain()
        matmul()

    @pl.when(t == n_steps - 1)
    def _():
        plain()


def _rope_operands(base_pos, n_base, row_pos):
    half = RET_DK // 2
    inv = np.exp(-math.log(ROPE_BASE) * np.arange(half, dtype=np.float64) / half)
    sign = np.concatenate([-np.ones(half), np.ones(half)])

    def tables(pos):
        ang = np.asarray(pos, np.float64)[:, None] * inv[None, :]
        cos = np.concatenate([np.cos(ang)] * 2, axis=1)
        sin = np.concatenate([np.sin(ang)] * 2, axis=1)
        return cos, sin, sin * sign

    out_a = [jnp.asarray(x.reshape(n_base, 1, RET_DK), F32) for x in tables(base_pos)]
    out_b = [jnp.asarray(x, F32) for x in tables(row_pos)]
    return out_a + out_b


def _inproj(h2d, w_bf, rope_ops, qg16, kg16, bd, tm, seq_len):
    emit_t = seq_len is not None
    T = h2d.shape[0]
    n_tok = T // tm
    n_rope = rope_ops[0].shape[0]
    n_mm = n_tok * N_COL_TILES
    n_steps = n_mm + 1

    def mm_tile(t):
        tt = jnp.minimum(t, n_mm - 1)
        return tt // n_tok, tt % n_tok

    def ep_tile(t):
        tt = jnp.maximum(t, 1) - 1
        return tt // n_tok, tt % n_tok

    def side_tile(t, col):
        j, i = ep_tile(t)
        return jnp.where(j < col, 0, jnp.where(j > col, n_tok - 1, i))

    base_spec = pl.BlockSpec((1, 1, RET_DK), lambda t: (ep_tile(t)[1] % n_rope, 0, 0))
    row_spec = pl.BlockSpec((tm, RET_DK), lambda t: (0, 0))
    in_specs = [
        pl.BlockSpec((tm, D_MODEL), lambda t: (mm_tile(t)[1], 0)),
        pl.BlockSpec((D_MODEL, COL_W), lambda t: (0, mm_tile(t)[0])),
        base_spec, base_spec, base_spec, row_spec, row_spec, row_spec,
        pl.BlockSpec((1, COL_W), lambda t: (0, 0)),
        pl.BlockSpec((1, COL_W), lambda t: (0, 0)),
        pl.BlockSpec((V7X_MXU_DIM, V7X_MXU_DIM), lambda t: (0, 0)),
    ]
    if emit_t:
        spb = seq_len // tm
        nk_shape = jax.ShapeDtypeStruct((T // seq_len, COL_W, seq_len), F32)
        nk_spec = pl.BlockSpec((1, COL_W, tm), lambda t: (side_tile(t, COL_KD) // spb, 0,
                                                          side_tile(t, COL_KD) % spb))
    else:
        nk_shape = jax.ShapeDtypeStruct((T, COL_W), F32)
        nk_spec = pl.BlockSpec((tm, COL_W), lambda t: (side_tile(t, COL_KD), 0))
    out_shape = [
        jax.ShapeDtypeStruct((T, N_COL_TILES * COL_W), BF16),
        nk_shape,
        jax.ShapeDtypeStruct((T * DA_HEADS, RET_DV), F32),
    ]
    out_specs = [
        pl.BlockSpec((tm, COL_W), lambda t: (ep_tile(t)[1], ep_tile(t)[0])),
        nk_spec,
        pl.BlockSpec((tm * DA_HEADS, RET_DV), lambda t: (side_tile(t, COL_VD), 0)),
    ]
    if emit_t:
        ab = tm // TOKEN_TILE
        out_shape += [jax.ShapeDtypeStruct((T // TOKEN_TILE, COL_W, TOKEN_TILE), BF16)] * 2
        out_specs += [
            pl.BlockSpec((ab, COL_W, TOKEN_TILE), lambda t: (side_tile(t, COL_QD), 0, 0)),
            pl.BlockSpec((ab, COL_W, TOKEN_TILE), lambda t: (side_tile(t, COL_VD), 0, 0))]
    return pl.pallas_call(
        functools.partial(_inproj_kernel, emit_t=emit_t, n_steps=n_steps, n_tok=n_tok),
        grid=(n_steps,), in_specs=in_specs, out_specs=out_specs, out_shape=out_shape,
        scratch_shapes=[pltpu.VMEM((tm, COL_W), F32)],
        compiler_params=_cparams(1), name="inproj_t" if emit_t else "inproj",
    )(h2d, w_bf, *rope_ops, qg16, kg16, bd)


def _ret_kernel(q_ref, k_ref, v_ref, z_ref, gn_ref, *rest, C, has_init):
    if has_init:
        s0_ref, y_ref, s_ref, dm_scr, rd_scr, kd_scr = rest
    else:
        y_ref, s_ref, dm_scr, rd_scr, kd_scr = rest
    b = pl.program_id(0)
    c = pl.program_id(1)

    @pl.when((b == 0) & (c == 0))
    def _():
        li = lax.broadcasted_iota(jnp.int32, (C, C), 0)
        mi = lax.broadcasted_iota(jnp.int32, (C, C), 1)
        causal = li >= mi
        diff = jnp.where(causal, (li - mi).astype(F32), 0.0)
        row = lax.broadcasted_iota(jnp.int32, (C, RET_DK), 0).astype(F32)
        for h in range(RET_HEADS):
            dm_scr[h] = jnp.where(causal, jnp.exp(diff * _LOG_G[h]), 0.0)
            rd_scr[h] = jnp.exp((row + 1.0) * _LOG_G[h])
            kd_scr[h] = jnp.exp((C - 1.0 - row) * _LOG_G[h])

    @pl.when(c == 0)
    def _():
        if has_init:
            s_ref[...] = s0_ref[...]
        else:
            s_ref[...] = jnp.zeros(s_ref.shape, F32)

    for h in range(RET_HEADS):
        sl = slice(h * RET_DK, (h + 1) * RET_DK)
        q = q_ref[:, sl]
        k = k_ref[:, sl]
        v = v_ref[:, sl]
        state = s_ref[0, h]
        sc = lax.dot_general(q, k, (((1,), (1,)), ((), ())), preferred_element_type=F32)
        sc = sc * dm_scr[h]
        intra = jnp.dot(sc.astype(BF16), v, preferred_element_type=F32)
        inter = jnp.dot(q, state.astype(BF16), preferred_element_type=F32) * rd_scr[h]
        o = intra + inter
        kw = (k.astype(F32) * kd_scr[h]).astype(BF16)
        upd = lax.dot_general(kw, v, (((0,), (0,)), ((), ())), preferred_element_type=F32)
        s_ref[0, h] = state * math.exp(C * _LOG_G[h]) + upd
        ms = jnp.mean(o * o, axis=-1, keepdims=True)
        z = z_ref[:, sl].astype(F32)
        y = o * lax.rsqrt(ms + NORM_EPS) * gn_ref[:, sl] * (z * jax.nn.sigmoid(z))
        y_ref[:, sl] = y.astype(BF16)


def _retention(p, ret_gn_row, n_seq, seq_len, C, init_state):
    n_chunks = seq_len // C
    has_init = init_state is not None

    def col(t):
        return pl.BlockSpec((C, COL_W), lambda b, c: (b * n_chunks + c, t))

    in_specs = [col(COL_QR), col(COL_KR), col(COL_VR), col(COL_ZR),
                pl.BlockSpec((1, COL_W), lambda b, c: (0, 0))]
    args = [p, p, p, p, ret_gn_row]
    state_spec = pl.BlockSpec((1, RET_HEADS, RET_DK, RET_DV), lambda b, c: (b, 0, 0, 0))
    if has_init:
        in_specs.append(state_spec)
        args.append(init_state)
    return pl.pallas_call(
        functools.partial(_ret_kernel, C=C, has_init=has_init),
        grid=(n_seq, n_chunks), in_specs=in_specs,
        out_specs=[pl.BlockSpec((C, COL_W), lambda b, c: (b * n_chunks + c, 0)), state_spec],
        out_shape=[jax.ShapeDtypeStruct((n_seq * seq_len, COL_W), BF16),
                   jax.ShapeDtypeStruct((n_seq, RET_HEADS, RET_DK, RET_DV), F32)],
        scratch_shapes=[pltpu.VMEM((RET_HEADS, C, C), F32),
                        pltpu.VMEM((RET_HEADS, C, RET_DK), F32),
                        pltpu.VMEM((RET_HEADS, C, RET_DK), F32)],
        compiler_params=_cparams(2), name="retention_init" if has_init else "retention",
    )(*args)


def _lambda_value(lq1, lk1, lq2, lk2, lam_init):
    a = jnp.sum(lq1[...] * lk1[...], axis=-1, keepdims=True)
    b = jnp.sum(lq2[...] * lk2[...], axis=-1, keepdims=True)
    return jnp.exp(a) - jnp.exp(b) + lam_init


def _bf16_split3(x):
    out, r = [], np.float64(x)
    for _ in range(3):
        c = np.float64(np.asarray(r, np.float32).astype(jnp.bfloat16).astype(np.float32))
        out.append(float(c))
        r = r - c
    return out


def _alibi_operands(blk):
    c = _bf16_split3(LOG2E)
    pos = np.arange(blk)
    ek = np.zeros((blk, 2 * DA_HEAD_DIM), np.float32)
    bq = np.zeros((2 * DA_HEAD_DIM, blk), np.float32)
    for t in range(3):
        ek[:, t] = pos >> 4
        ek[:, 3 + t] = pos & 15
        ek[:, 6 + t] = -16.0 * c[t]
        ek[:, 9 + t] = -c[t]
        bq[t, :] = 16.0 * c[t]
        bq[3 + t, :] = c[t]
        bq[6 + t, :] = pos >> 4
        bq[9 + t, :] = pos & 15
    return jnp.asarray(ek, BF16), jnp.asarray(bq, BF16)


def _attn_kernel(slog2_ref, spow2_ref, qt1_ref, qt2_ref, k1_ref, k2_ref, vt_ref, ek_ref, bq_ref,
                 lq1, lk1, lq2, lk2, gn_ref, o_ref,
                 qx_scr, m_scr, l_scr, acc_scr, s_scr, mb_scr, dbias_scr, *, blk, lam_init):
    j = pl.program_id(1)
    qi = pl.program_id(2)
    half = DA_HEAD_DIM
    hd2 = 2 * DA_HEAD_DIM
    k_refs = (k1_ref, k2_ref)

    zeros = jnp.zeros((half, blk), BF16)
    for hh in range(2):
        bias_rows = (bq_ref[...].astype(F32) * spow2_ref[2 * j + hh]).astype(BF16)
        for c, qref in enumerate((qt1_ref, qt2_ref)):
            idx = 2 * c + hh
            if hh == 0:
                qx_scr[idx, :half, :] = qref[0, :half, :]
                qx_scr[idx, half:hd2, :] = zeros
            else:
                qx_scr[idx, :half, :] = zeros
                qx_scr[idx, half:hd2, :] = qref[0, half:, :]
            qx_scr[idx, hd2:, :] = bias_rows
    m_scr[...] = jnp.full(m_scr.shape, _NEG, F32)
    l_scr[...] = jnp.zeros(l_scr.shape, F32)
    acc_scr[...] = jnp.zeros(acc_scr.shape, F32)

    @pl.when(qi == 0)
    def _():
        kk = lax.broadcasted_iota(jnp.int32, (blk, blk), 0)
        qq = lax.broadcasted_iota(jnp.int32, (blk, blk), 1)
        allowed = (kk // CHUNK) <= (qq // CHUNK)
        dist = jnp.abs(kk - qq).astype(F32)
        for hh in range(2):
            dbias_scr[hh] = jnp.where(allowed, -slog2_ref[2 * j + hh] * dist, _NEG)

    def score_diag(par):
        row0 = pl.multiple_of(qi * blk, blk)
        for c in range(2):
            kc = k_refs[c][pl.ds(row0, blk), :]
            for hh in range(2):
                idx = 2 * c + hh
                s = jnp.dot(kc, qx_scr[idx, :hd2, :], preferred_element_type=F32) + dbias_scr[hh]
                s_scr[par, idx] = s
                mb_scr[par, idx] = jnp.max(s, axis=0, keepdims=True)

    qw = blk // Q_SPLIT
    tiles = [(idx, pl.ds(h * qw, qw)) for idx in range(4) for h in range(Q_SPLIT)]

    def score_tile(ki, par, idx, qs):
        row0 = pl.multiple_of(ki * blk, blk)
        kx = jnp.concatenate([k_refs[idx // 2][pl.ds(row0, blk), :], ek_ref[...]], axis=1)
        s = jnp.dot(kx, qx_scr[idx, :, qs], preferred_element_type=F32)
        s_scr[par, idx, :, qs] = s
        mb_scr[par, idx, :, qs] = jnp.max(s, axis=0, keepdims=True)

    def consume_tile(kb, par, idx, qs):
        hh = idx % 2
        shift = slog2_ref[2 * j + hh] * ((qi - kb) * blk).astype(F32)
        m_old = m_scr[idx, :, qs]
        m_new = jnp.maximum(m_old, mb_scr[par, idx, :, qs] - shift)
        alpha = jnp.exp2(m_old - m_new)
        p = jnp.exp2(s_scr[par, idx, :, qs] - (m_new + shift))
        l_scr[idx, :, qs] = alpha * l_scr[idx, :, qs] + jnp.sum(p, axis=0, keepdims=True)
        vt_h = vt_ref[kb, hh * hd2:(hh + 1) * hd2, :]
        acc_scr[idx, :, qs] = alpha * acc_scr[idx, :, qs] + jnp.dot(
            vt_h, p.astype(BF16), preferred_element_type=F32)
        m_scr[idx, :, qs] = m_new

    def consume(kb, par):
        for idx, qs in tiles:
            consume_tile(kb, par, idx, qs)

    def stage(ki, par, kb):
        for idx, qs in tiles:
            score_tile(ki, par, idx, qs)
            consume_tile(kb, 1 - par, idx, qs)

    score_diag(0)

    def pair_body(u, carry):
        t0 = 2 * u
        stage(t0, 1, jnp.where(t0 == 0, qi, t0 - 1))
        stage(t0 + 1, 0, t0)
        return carry

    lax.fori_loop(0, qi // 2, pair_body, 0)

    @pl.when(qi % 2 == 1)
    def _():
        t0 = qi - 1
        stage(t0, 1, jnp.where(t0 == 0, qi, t0 - 1))
        consume(t0, 1)

    @pl.when(qi % 2 == 0)
    def _():
        consume(jnp.where(qi == 0, qi, qi - 1), 0)

    lam = _lambda_value(lq1, lk1, lq2, lk2, lam_init)
    for hh in range(2):
        o = acc_scr[hh] / l_scr[hh] - lam * (acc_scr[2 + hh] / l_scr[2 + hh])
        ms = jnp.mean(o * o, axis=0, keepdims=True)
        on = (o * lax.rsqrt(ms + NORM_EPS)).T
        sl = slice(hh * hd2, (hh + 1) * hd2)
        o_ref[:, sl] = (on * gn_ref[:, sl] * (1.0 - lam_init)).astype(BF16)


def _attention_prompt(p, qt, vt, lam_rows, da_gn_row, n_seq, seq_len, blk, lam_init):
    nb = seq_len // blk
    n_hp = DA_HEADS // 2
    hd2 = 2 * DA_HEAD_DIM
    kcol0 = COL_KD * COL_W // hd2
    ek, bq = _alibi_operands(blk)

    smem = pl.BlockSpec(memory_space=pltpu.SMEM)
    in_specs = [
        smem, smem,
        pl.BlockSpec((1, hd2, blk), lambda b, j, q: (b * nb + q, j, 0)),
        pl.BlockSpec((1, hd2, blk), lambda b, j, q: (b * nb + q, n_hp + j, 0)),
        pl.BlockSpec((seq_len, hd2), lambda b, j, q: (b, kcol0 + j)),
        pl.BlockSpec((seq_len, hd2), lambda b, j, q: (b, kcol0 + n_hp + j)),
        pl.BlockSpec((nb, 2 * hd2, blk), lambda b, j, q: (b, j, 0)),
        pl.BlockSpec((blk, hd2), lambda b, j, q: (0, 0)),
        pl.BlockSpec((hd2, blk), lambda b, j, q: (0, 0)),
    ] + [pl.BlockSpec((1, DA_HEAD_DIM), lambda b, j, q: (0, 0))] * 4 + [
        pl.BlockSpec((1, 2 * hd2), lambda b, j, q: (0, j)),
    ]
    return pl.pallas_call(
        functools.partial(_attn_kernel, blk=blk, lam_init=lam_init),
        grid=(n_seq, n_hp, nb), in_specs=in_specs,
        out_specs=pl.BlockSpec((blk, 2 * hd2), lambda b, j, q: (b * nb + q, j)),
        out_shape=jax.ShapeDtypeStruct((n_seq * seq_len, COL_W), BF16),
        scratch_shapes=[pltpu.VMEM((4, 2 * hd2, blk), BF16),
                        pltpu.VMEM((4, 1, blk), F32),
                        pltpu.VMEM((4, 1, blk), F32),
                        pltpu.VMEM((4, hd2, blk), F32),
                        pltpu.VMEM((2, 4, blk, blk), F32),
                        pltpu.VMEM((2, 4, 1, blk), F32),
                        pltpu.VMEM((2, blk, blk), F32)],
        compiler_params=_cparams(3), name="diff_attn_prompt",
    )(jnp.asarray(_SLOPES_LOG2), jnp.asarray(_SLOPES), qt, qt, p, p, vt, ek, bq,
      *lam_rows, da_gn_row)


def _sattn_kernel(slopes_ref, q_ref, kn_ref, vn_ref, kc_ref, vc_ref,
                  lq1, lk1, lq2, lk2, gn_ref, o_ref, *, L, P, lam_init):
    lam = _lambda_value(lq1, lk1, lq2, lk2, lam_init)
    hd2 = 2 * DA_HEAD_DIM
    lane = lax.broadcasted_iota(jnp.int32, (L, hd2), 1)
    qpos_c = P + lax.broadcasted_iota(jnp.int32, (L, P), 0)
    kpos_c = lax.broadcasted_iota(jnp.int32, (L, P), 1)
    qpos_n = P + lax.broadcasted_iota(jnp.int32, (L, L), 0)
    kpos_n = P + lax.broadcasted_iota(jnp.int32, (L, L), 1)
    dist_c = jnp.abs(qpos_c - kpos_c).astype(F32)
    dist_n = jnp.abs(qpos_n - kpos_n).astype(F32)
    all_allowed = P % CHUNK == 0 and L <= CHUNK
    ok_c = (kpos_c // CHUNK) <= (qpos_c // CHUNK)
    ok_n = (kpos_n // CHUNK) <= (qpos_n // CHUNK)
    nt = (((1,), (1,)), ((), ()))
    for pr in range(DA_HEADS // 2):
        sls = [slice((c * (DA_HEADS // 2) + pr) * hd2, (c * (DA_HEADS // 2) + pr + 1) * hd2)
               for c in range(2)]
        q128s = [q_ref[:, sl].astype(F32) * _Q_SCALE_LOG2 for sl in sls]
        kc_ts = [kc_ref[0, sl, :].astype(BF16) for sl in sls]
        for hh in range(2):
            h = 2 * pr + hh
            slope = slopes_ref[h]
            probs = []
            for c in range(2):
                keep = (lane < DA_HEAD_DIM) if hh == 0 else (lane >= DA_HEAD_DIM)
                qm = jnp.where(keep, q128s[c], 0.0).astype(BF16)
                s_c = jnp.dot(qm, kc_ts[c], preferred_element_type=F32) - slope * dist_c
                s_n = lax.dot_general(qm, kn_ref[:, sls[c]], nt,
                                      preferred_element_type=F32) - slope * dist_n
                if not all_allowed:
                    s_c = jnp.where(ok_c, s_c, _NEG)
                    s_n = jnp.where(ok_n, s_n, _NEG)
                m = jnp.maximum(jnp.max(s_c, axis=-1, keepdims=True),
                                jnp.max(s_n, axis=-1, keepdims=True))
                p_c = jnp.exp2(s_c - m)
                p_n = jnp.exp2(s_n - m)
                inv = 1.0 / (jnp.sum(p_c, axis=-1, keepdims=True)
                             + jnp.sum(p_n, axis=-1, keepdims=True))
                probs.append((p_c * inv, p_n * inv))
            a_c = (probs[0][0] - lam * probs[1][0]).astype(BF16)
            a_n = (probs[0][1] - lam * probs[1][1]).astype(BF16)
            vsl = slice(h * hd2, (h + 1) * hd2)
            vc = vc_ref[0, pl.ds(h, P, stride=DA_HEADS), :].astype(BF16)
            o = (jnp.dot(a_c, vc, preferred_element_type=F32)
                 + jnp.dot(a_n, vn_ref[:, vsl], preferred_element_type=F32))
            ms = jnp.mean(o * o, axis=-1, keepdims=True)
            on = o * lax.rsqrt(ms + NORM_EPS) * gn_ref[:, vsl] * (1.0 - lam_init)
            o_ref[:, vsl] = on.astype(BF16)


def _attention_sample(p, cache_kt, cache_v, lam_rows, da_gn_row, n_seq, L, lam_init):
    P = cache_kt.shape[2]
    smem = pl.BlockSpec(memory_space=pltpu.SMEM)
    in_specs = [
        smem,
        pl.BlockSpec((L, COL_W), lambda s: (s, COL_QD)),
        pl.BlockSpec((L, COL_W), lambda s: (s, COL_KD)),
        pl.BlockSpec((L, COL_W), lambda s: (s, COL_VD)),
        pl.BlockSpec((1, COL_W, P), lambda s: (s, 0, 0)),
        pl.BlockSpec((1, P * DA_HEADS, RET_DV), lambda s: (s, 0, 0)),
    ] + [pl.BlockSpec((1, DA_HEAD_DIM), lambda s: (0, 0))] * 4 + [
        pl.BlockSpec((1, COL_W), lambda s: (0, 0)),
    ]
    return pl.pallas_call(
        functools.partial(_sattn_kernel, L=L, P=P, lam_init=lam_init),
        grid=(n_seq,), in_specs=in_specs,
        out_specs=pl.BlockSpec((L, COL_W), lambda s: (s, 0)),
        out_shape=jax.ShapeDtypeStruct((n_seq * L, COL_W), BF16),
        compiler_params=_cparams(1), name="diff_attn_sample",
    )(jnp.asarray(_SLOPES_LOG2), p, p, p, cache_kt, cache_v, *lam_rows, da_gn_row)


def _merge_kernel(x_ref, yr_ref, yd_ref, gr0, gr1, gd0, gd1, wr_ref, wd_ref, wo_ref, gm_ref,
                  x1_ref, h2_ref):
    yr = jnp.dot(yr_ref[...], wr_ref[...], preferred_element_type=F32)
    yd = jnp.dot(yd_ref[...], wd_ref[...], preferred_element_type=F32)
    gr = jnp.concatenate([gr0[...], gr1[...]], axis=1).astype(F32)
    gd = jnp.concatenate([gd0[...], gd1[...]], axis=1).astype(F32)
    m = jax.nn.sigmoid(gr) * yr + jax.nn.sigmoid(gd) * yd
    x1 = x_ref[...] + jnp.dot(m.astype(BF16), wo_ref[...], preferred_element_type=F32)
    x1_ref[...] = x1
    ms = jnp.mean(x1 * x1, axis=-1, keepdims=True)
    h2_ref[...] = (x1 * lax.rsqrt(ms + NORM_EPS) * gm_ref[...]).astype(BF16)


def _merge(x2d, yr_in, yd_in, p, w_ret_o, w_da_o, w_out, g_mlp_row, tm):
    T = x2d.shape[0]

    def whole(shape):
        return pl.BlockSpec(shape, lambda i: (0, 0), pipeline_mode=pl.Buffered(1))

    def pcol(t):
        return pl.BlockSpec((tm, COL_W), lambda i: (i, t))

    in_specs = [
        pl.BlockSpec((tm, D_MODEL), lambda i: (i, 0)),
        pl.BlockSpec((tm, COL_W), lambda i: (i, 0)),
        pl.BlockSpec((tm, COL_W), lambda i: (i, 0)),
        pcol(COL_GR), pcol(COL_GR + 1), pcol(COL_GD), pcol(COL_GD + 1),
        whole((COL_W, D_MODEL)), whole((COL_W, D_MODEL)), whole((D_MODEL, D_MODEL)),
        pl.BlockSpec((1, D_MODEL), lambda i: (0, 0)),
    ]
    return pl.pallas_call(
        _merge_kernel, grid=(T // tm,), in_specs=in_specs,
        out_specs=[pl.BlockSpec((tm, D_MODEL), lambda i: (i, 0)),
                   pl.BlockSpec((tm, D_MODEL), lambda i: (i, 0))],
        out_shape=[jax.ShapeDtypeStruct((T, D_MODEL), F32),
                   jax.ShapeDtypeStruct((T, D_MODEL), BF16)],
        compiler_params=_cparams(1), name="merge",
    )(x2d, yr_in, yd_in, p, p, p, p, w_ret_o, w_da_o, w_out, g_mlp_row)


def _mlp_kernel(x1_ref, h2_ref, wu_ref, wd_ref, y_ref, r_scr, *, n_steps, nf):
    t = pl.program_id(0)
    fb = (jnp.maximum(t, 1) - 1) % nf

    def up(par):
        u = jnp.dot(h2_ref[...], wu_ref[...], preferred_element_type=F32)
        r_scr[par] = jnp.square(jnp.maximum(u, 0.0)).astype(BF16)

    def down(par, first_chunk):
        contrib = jnp.dot(r_scr[par], wd_ref[...], preferred_element_type=F32)
        if first_chunk:
            y_ref[...] = x1_ref[...] + contrib
        else:
            y_ref[...] += contrib

    mid = (t > 0) & (t < n_steps - 1)

    @pl.when(t == 0)
    def _():
        up(0)

    @pl.when(mid & (t % 2 == 0))
    def _():
        down(1, False)
        up(0)

    @pl.when(mid & (t % 2 == 1) & (fb == 0))
    def _():
        down(0, True)
        up(1)

    @pl.when(mid & (t % 2 == 1) & (fb != 0))
    def _():
        down(0, False)
        up(1)

    @pl.when(t == n_steps - 1)
    def _():
        down((n_steps - 2) % 2, False)


def _mlp(x1, h2, w_up, w_down, tm, tf):
    T = x1.shape[0]
    nf = D_FF // tf
    assert nf % 2 == 0 and nf >= 2
    n_up = (T // tm) * nf
    n_steps = n_up + 1

    def up_idx(t):
        return jnp.minimum(t, n_up - 1)

    def down_idx(t):
        return jnp.maximum(t, 1) - 1

    return pl.pallas_call(
        functools.partial(_mlp_kernel, n_steps=n_steps, nf=nf), grid=(n_steps,),
        in_specs=[pl.BlockSpec((tm, D_MODEL), lambda t: (down_idx(t) // nf, 0)),
                  pl.BlockSpec((tm, D_MODEL), lambda t: (up_idx(t) // nf, 0)),
                  pl.BlockSpec((D_MODEL, tf), lambda t: (0, up_idx(t) % nf)),
                  pl.BlockSpec((tf, D_MODEL), lambda t: (down_idx(t) % nf, 0))],
        out_specs=pl.BlockSpec((tm, D_MODEL), lambda t: (down_idx(t) // nf, 0)),
        out_shape=jax.ShapeDtypeStruct((T, D_MODEL), F32),
        scratch_shapes=[pltpu.VMEM((2, tm, tf), BF16)],
        compiler_params=_cparams(1), name="mlp",
    )(x1, h2, w_up, w_down)


def _block_diag_mean():
    r = np.arange(V7X_MXU_DIM) // DA_HEAD_DIM
    return jnp.asarray((r[:, None] == r[None, :]).astype(np.float32) / DA_HEAD_DIM, BF16)


def kernel(x_prompt, x_sample, cache_da_k, cache_da_v, state_ret, g_mix, w_in, ret_gn, qn_g, kn_g,
           lambda_q1, lambda_k1, lambda_q2, lambda_k2, da_gn, w_ret_o, w_da_o, w_out, g_mlp,
           w_up, w_down):
    B, S, _ = x_prompt.shape
    DB, L, _ = x_sample.shape
    P = cache_da_k.shape[2]
    depth = w_in.shape[0]
    assert S % TOKEN_TILE == 0 and S % RET_CHUNK_PROMPT == 0 and L % 16 == 0

    tm_p = TOKEN_TILE
    tm_s = _pick_tile(DB * L, TOKEN_TILE)
    tm_ip = _pick_tile(S, INPROJ_TILE)
    tm_is = _pick_tile(DB * L, INPROJ_TILE)
    assert tm_ip % TOKEN_TILE == 0 and (tm_is % L == 0 or L % tm_is == 0)
    rope_p = _rope_operands(np.arange(S // tm_ip) * tm_ip, S // tm_ip, np.arange(tm_ip))
    if tm_is >= L:
        rope_s = _rope_operands([0], 1, P + np.arange(tm_is) % L)
    else:
        rope_s = _rope_operands(P + np.arange(L // tm_is) * tm_is, L // tm_is, np.arange(tm_is))
    bd = _block_diag_mean()
    n_heads_qk = COL_W // DA_HEAD_DIM

    xp = x_prompt.reshape(B * S, D_MODEL)
    xs = x_sample.reshape(DB * L, D_MODEL)
    kp_l, vp_l, rp_l, ks_l, vs_l, rs_l = [], [], [], [], [], []
    for l in range(depth):
        lam_init = 0.8 - 0.6 * math.exp(-0.3 * l)
        w_in_bf = w_in[l].astype(BF16)
        w_ro, w_do, w_o = w_ret_o[l].astype(BF16), w_da_o[l].astype(BF16), w_out[l].astype(BF16)
        w_u, w_d = w_up[l].astype(BF16), w_down[l].astype(BF16)
        g_mix_row = g_mix[l].reshape(1, D_MODEL)
        g_mlp_row = g_mlp[l].reshape(1, D_MODEL)
        qg16 = jnp.tile(qn_g[l], n_heads_qk).reshape(1, COL_W)
        kg16 = jnp.tile(kn_g[l], n_heads_qk).reshape(1, COL_W)
        ret_gn_row = ret_gn[l].reshape(1, COL_W)
        da_gn_row = da_gn[l].reshape(1, COL_W)
        lam_rows = [a[l].astype(F32).reshape(1, DA_HEAD_DIM)
                    for a in (lambda_q1, lambda_k1, lambda_q2, lambda_k2)]

        p, nkt, nv, qt, vt = _inproj(_rmsnorm(xp, g_mix_row, tm_ip), w_in_bf, rope_p,
                                     qg16, kg16, bd, tm_ip, S)
        yr_in, s_fin = _retention(p, ret_gn_row, B, S, RET_CHUNK_PROMPT, None)
        yd_in = _attention_prompt(p, qt, vt, lam_rows, da_gn_row, B, S, TOKEN_TILE, lam_init)
        x1, h2 = _merge(xp, yr_in, yd_in, p, w_ro, w_do, w_o, g_mlp_row, tm_p)
        xp = _mlp(x1, h2, w_u, w_d, tm_p, MLP_FF_TILE)
        nk = jnp.transpose(nkt.reshape(B, 2 * DA_HEADS, DA_HEAD_DIM, S), (0, 3, 1, 2))
        kp_l.append(nk)
        vp_l.append(nv.reshape(B, S, DA_HEADS, 2 * DA_HEAD_DIM))
        rp_l.append(s_fin)

        p, nk, nv = _inproj(_rmsnorm(xs, g_mix_row, tm_is), w_in_bf, rope_s, qg16, kg16, bd,
                            tm_is, None)
        yr_in, s_new = _retention(p, ret_gn_row, DB, L, L, state_ret[l].astype(F32))
        cache_kt = jnp.transpose(cache_da_k[l].reshape(DB, P, COL_W), (0, 2, 1))
        yd_in = _attention_sample(p, cache_kt, cache_da_v[l].reshape(DB, P * DA_HEADS, RET_DV),
                                  lam_rows, da_gn_row, DB, L, lam_init)
        x1, h2 = _merge(xs, yr_in, yd_in, p, w_ro, w_do, w_o, g_mlp_row, tm_s)
        xs = _mlp(x1, h2, w_u, w_d, tm_s, MLP_FF_TILE)
        ks_l.append(nk.reshape(DB, L, 2 * DA_HEADS, DA_HEAD_DIM))
        vs_l.append(nv.reshape(DB, L, DA_HEADS, 2 * DA_HEAD_DIM))
        rs_l.append(s_new)

    return (xp.reshape(B, S, D_MODEL), xs.reshape(DB, L, D_MODEL),
            jnp.stack(kp_l, 0), jnp.stack(vp_l, 0), jnp.stack(rp_l, 0),
            jnp.stack(ks_l, 0), jnp.stack(vs_l, 0), jnp.stack(rs_l, 0))
```

```python
import functools
import math

import numpy as np
import jax
import jax.numpy as jnp
from jax import lax
from jax.experimental import pallas as pl
from jax.experimental.pallas import tpu as pltpu

D_MODEL = 2048
CHUNK = 64
RET_HEADS = 8
RET_DK = 128
RET_DV = 128
DA_HEADS = 8
DA_HEAD_DIM = 64
D_FF = 4 * D_MODEL
ROPE_BASE = 10000.0
NORM_EPS = 1e-6
LOG2E = 1.4426950408889634

V7X_MXU_DIM = 256
V7X_VMEM_LIMIT_BYTES = 56 * 1024 * 1024

COL_W = 1024
N_COL_TILES = 11
COL_QR, COL_KR, COL_VR, COL_ZR, COL_QD, COL_KD, COL_VD = 0, 1, 2, 3, 4, 5, 6
COL_GR, COL_GD = 7, 9

TOKEN_TILE = 512
INPROJ_TILE = 1024
RET_CHUNK_PROMPT = 256
MLP_FF_TILE = 1024
Q_SPLIT = 2

F32 = jnp.float32
BF16 = jnp.bfloat16

_LOG_G = [math.log1p(-(2.0 ** (-5.0 - h))) for h in range(RET_HEADS)]
_SLOPES = np.asarray([2.0 ** (-8.0 * (h + 1) / DA_HEADS) for h in range(DA_HEADS)], np.float32)
_SLOPES_LOG2 = (_SLOPES * np.float32(LOG2E)).astype(np.float32)
_Q_SCALE_LOG2 = (DA_HEAD_DIM ** -0.5) * LOG2E
_NEG = -1e30


def _cparams(n_grid):
    return pltpu.CompilerParams(
        dimension_semantics=("arbitrary",) * n_grid,
        vmem_limit_bytes=V7X_VMEM_LIMIT_BYTES)


def _pick_tile(n, cap):
    t = cap
    while n % t:
        t //= 2
    return t


def _inproj0_kernel(x_ref, g_ref, w_ref, ca_ref, sa_ref, sas_ref, cb_ref, sb_ref, sbs_ref,
                    h_ref, p_ref, hs_scr, s_scr, *, n_tok):
    t = pl.program_id(0)

    def norm(par):
        x = x_ref[...]
        ms = jnp.mean(x * x, axis=-1, keepdims=True)
        h = (x * lax.rsqrt(ms + NORM_EPS) * g_ref[...]).astype(BF16)
        h_ref[...] = h
        hs_scr[par] = h

    def matmul(par):
        s_scr[...] = jnp.dot(hs_scr[par], w_ref[...], preferred_element_type=F32)

    def rope():
        ca, sa, sas = ca_ref[0], sa_ref[0], sas_ref[0]
        cos = ca * cb_ref[...] - sa * sb_ref[...]
        sin = sas * cb_ref[...] + ca * sbs_ref[...]
        for hh in range(RET_HEADS):
            sl = slice(hh * RET_DK, (hh + 1) * RET_DK)
            a = s_scr[:, sl]
            p_ref[:, sl] = (a * cos + pltpu.roll(a, RET_DK // 2, axis=1) * sin).astype(BF16)

    @pl.when(t == 0)
    def _():
        norm(0)

    @pl.when(t == 1)
    def _():
        matmul(0)
        if n_tok > 1:
            norm(1)

    for par in range(2):
        @pl.when((t >= 2) & (t < n_tok) & (t % 2 == par))
        def _(par=par):
            rope()
            matmul(1 - par)
            norm(par)

    if n_tok >= 2:
        @pl.when(t == n_tok)
        def _():
            rope()
            matmul((n_tok - 1) % 2)

    @pl.when(t == n_tok + 1)
    def _():
        rope()


def _inproj0(x2d, g_row, w_bf, rope_ops, tm):
    T = x2d.shape[0]
    n_tok = T // tm
    n_rope = rope_ops[0].shape[0]
    last = n_tok - 1

    def ep(t):
        return jnp.maximum(t, 2) - 2

    base_spec = pl.BlockSpec((1, 1, RET_DK), lambda t: (ep(t) % n_rope, 0, 0))
    row_spec = pl.BlockSpec((tm, RET_DK), lambda t: (0, 0))
    return pl.pallas_call(
        functools.partial(_inproj0_kernel, n_tok=n_tok), grid=(n_tok + 2,),
        in_specs=[pl.BlockSpec((tm, D_MODEL), lambda t: (jnp.minimum(t, last), 0)),
                  pl.BlockSpec((1, D_MODEL), lambda t: (0, 0)),
                  pl.BlockSpec((D_MODEL, COL_W), lambda t: (0, COL_QR)),
                  base_spec, base_spec, base_spec, row_spec, row_spec, row_spec],
        out_specs=[pl.BlockSpec((tm, D_MODEL), lambda t: (jnp.minimum(t, last), 0)),
                   pl.BlockSpec((tm, COL_W), lambda t: (ep(t), COL_QR))],
        out_shape=[jax.ShapeDtypeStruct((T, D_MODEL), BF16),
                   jax.ShapeDtypeStruct((T, N_COL_TILES * COL_W), BF16)],
        scratch_shapes=[pltpu.VMEM((2, tm, D_MODEL), BF16), pltpu.VMEM((tm, COL_W), F32)],
        compiler_params=_cparams(1), name="inproj_first",
    )(x2d, g_row, w_bf, *rope_ops)


def _inproj_kernel(h_ref, w_ref, ca_ref, sa_ref, sas_ref, cb_ref, sb_ref, sbs_ref,
                   qg_ref, kg_ref, bd_ref, p_in_ref,
                   p_ref, nk_ref, nv_ref, *rest, emit_t, n_steps, n_tok):
    if emit_t:
        qt_ref, vt_ref, s_scr = rest
    else:
        (s_scr,) = rest
    del p_in_ref
    t = pl.program_id(0)
    jb = COL_KR + (jnp.maximum(t, 1) - 1) // n_tok

    def matmul():
        s_scr[...] = jnp.dot(h_ref[...], w_ref[...], preferred_element_type=F32)

    def rope(scale):
        ca, sa, sas = ca_ref[0], sa_ref[0], sas_ref[0]
        cos = ca * cb_ref[...] - sa * sb_ref[...]
        sin = sas * cb_ref[...] + ca * sbs_ref[...]
        for hh in range(RET_HEADS):
            sl = slice(hh * RET_DK, (hh + 1) * RET_DK)
            a = s_scr[:, sl]
            o = a * cos + pltpu.roll(a, RET_DK // 2, axis=1) * sin
            if scale != 1.0:
                o = o * scale
            p_ref[:, sl] = o.astype(BF16)

    def plain():
        p_ref[...] = s_scr[...].astype(BF16)

    def store_feature_major(ref, val_t):
        for b in range(ref.shape[0]):
            ref[b] = val_t[:, b * TOKEN_TILE:(b + 1) * TOKEN_TILE]

    def head_norm(acc, gain_row):
        sq = (acc * acc).astype(BF16)
        parts = [
            jnp.dot(sq[:, c * V7X_MXU_DIM:(c + 1) * V7X_MXU_DIM], bd_ref[...],
                    preferred_element_type=F32)
            for c in range(COL_W // V7X_MXU_DIM)
        ]
        ms = jnp.concatenate(parts, axis=1)
        return acc * lax.rsqrt(ms + NORM_EPS) * gain_row

    def q_diff():
        n = head_norm(s_scr[...], qg_ref[...])
        p_ref[...] = n.astype(BF16)
        if emit_t:
            store_feature_major(qt_ref, (n * _Q_SCALE_LOG2).T.astype(BF16))

    def k_diff():
        n = head_norm(s_scr[...], kg_ref[...])
        if emit_t:
            nk_ref[0] = n.T
        else:
            nk_ref[...] = n
        p_ref[...] = n.astype(BF16)

    def v_diff():
        acc = s_scr[...]
        for h in range(DA_HEADS):
            nv_ref[pl.ds(h, acc.shape[0], stride=DA_HEADS), :] = acc[:, h * RET_DV:(h + 1) * RET_DV]
        p_ref[...] = acc.astype(BF16)
        if emit_t:
            store_feature_major(vt_ref, acc.T.astype(BF16))

    epilogues = {COL_KR: lambda: rope(RET_DK ** -0.5), COL_QD: q_diff, COL_KD: k_diff,
                 COL_VD: v_diff}
    mid = (t > 0) & (t < n_steps - 1)

    @pl.when(t == 0)
    def _():
        matmul()

    for col, epilogue in epilogues.items():
        @pl.when(mid & (jb == col))
        def _(epilogue=epilogue):
            epilogue()
            matmul()

    @pl.when(mid & ((jb == COL_VR) | (jb == COL_ZR) | (jb >= COL_GR)))
    def _():
        plain()
        matmul()

    @pl.when(t == n_steps - 1)
    def _():
        plain()


def _rope_operands(base_pos, n_base, row_pos):
    half = RET_DK // 2
    inv = np.exp(-math.log(ROPE_BASE) * np.arange(half, dtype=np.float64) / half)
    sign = np.concatenate([-np.ones(half), np.ones(half)])

    def tables(pos):
        ang = np.asarray(pos, np.float64)[:, None] * inv[None, :]
        cos = np.concatenate([np.cos(ang)] * 2, axis=1)
        sin = np.concatenate([np.sin(ang)] * 2, axis=1)
        return cos, sin, sin * sign

    out_a = [jnp.asarray(x.reshape(n_base, 1, RET_DK), F32) for x in tables(base_pos)]
    out_b = [jnp.asarray(x, F32) for x in tables(row_pos)]
    return out_a + out_b


def _inproj(h2d, p_first, w_bf, rope_ops, qg16, kg16, bd, tm, seq_len):
    emit_t = seq_len is not None
    T = h2d.shape[0]
    n_tok = T // tm
    n_rope = rope_ops[0].shape[0]
    n_mm = n_tok * (N_COL_TILES - COL_KR)
    n_steps = n_mm + 1

    def mm_tile(t):
        tt = jnp.minimum(t, n_mm - 1)
        return COL_KR + tt // n_tok, tt % n_tok

    def ep_tile(t):
        tt = jnp.maximum(t, 1) - 1
        return COL_KR + tt // n_tok, tt % n_tok

    def side_tile(t, col):
        j, i = ep_tile(t)
        return jnp.where(j < col, 0, jnp.where(j > col, n_tok - 1, i))

    base_spec = pl.BlockSpec((1, 1, RET_DK), lambda t: (ep_tile(t)[1] % n_rope, 0, 0))
    row_spec = pl.BlockSpec((tm, RET_DK), lambda t: (0, 0))
    in_specs = [
        pl.BlockSpec((tm, D_MODEL), lambda t: (mm_tile(t)[1], 0)),
        pl.BlockSpec((D_MODEL, COL_W), lambda t: (0, mm_tile(t)[0])),
        base_spec, base_spec, base_spec, row_spec, row_spec, row_spec,
        pl.BlockSpec((1, COL_W), lambda t: (0, 0)),
        pl.BlockSpec((1, COL_W), lambda t: (0, 0)),
        pl.BlockSpec((V7X_MXU_DIM, V7X_MXU_DIM), lambda t: (0, 0)),
        pl.BlockSpec(memory_space=pl.ANY),
    ]
    if emit_t:
        spb = seq_len // tm
        nk_shape = jax.ShapeDtypeStruct((T // seq_len, COL_W, seq_len), F32)
        nk_spec = pl.BlockSpec((1, COL_W, tm), lambda t: (side_tile(t, COL_KD) // spb, 0,
                                                          side_tile(t, COL_KD) % spb))
    else:
        nk_shape = jax.ShapeDtypeStruct((T, COL_W), F32)
        nk_spec = pl.BlockSpec((tm, COL_W), lambda t: (side_tile(t, COL_KD), 0))
    out_shape = [
        jax.ShapeDtypeStruct((T, N_COL_TILES * COL_W), BF16),
        nk_shape,
        jax.ShapeDtypeStruct((T * DA_HEADS, RET_DV), F32),
    ]
    out_specs = [
        pl.BlockSpec((tm, COL_W), lambda t: (ep_tile(t)[1], ep_tile(t)[0])),
        nk_spec,
        pl.BlockSpec((tm * DA_HEADS, RET_DV), lambda t: (side_tile(t, COL_VD), 0)),
    ]
    if emit_t:
        ab = tm // TOKEN_TILE
        out_shape += [jax.ShapeDtypeStruct((T // TOKEN_TILE, COL_W, TOKEN_TILE), BF16)] * 2
        out_specs += [
            pl.BlockSpec((ab, COL_W, TOKEN_TILE), lambda t: (side_tile(t, COL_QD), 0, 0)),
            pl.BlockSpec((ab, COL_W, TOKEN_TILE), lambda t: (side_tile(t, COL_VD), 0, 0))]
    return pl.pallas_call(
        functools.partial(_inproj_kernel, emit_t=emit_t, n_steps=n_steps, n_tok=n_tok),
        grid=(n_steps,), in_specs=in_specs, out_specs=out_specs, out_shape=out_shape,
        scratch_shapes=[pltpu.VMEM((tm, COL_W), F32)],
        input_output_aliases={11: 0},
        compiler_params=_cparams(1), name="inproj_t" if emit_t else "inproj",
    )(h2d, w_bf, *rope_ops, qg16, kg16, bd, p_first)


def _ret_kernel(q_ref, k_ref, v_ref, z_ref, gn_ref, *rest, C, has_init):
    if has_init:
        s0_ref, y_ref, s_ref, dm_scr, rd_scr, kd_scr = rest
    else:
        y_ref, s_ref, dm_scr, rd_scr, kd_scr = rest
    b = pl.program_id(0)
    c = pl.program_id(1)

    @pl.when((b == 0) & (c == 0))
    def _():
        li = lax.broadcasted_iota(jnp.int32, (C, C), 0)
        mi = lax.broadcasted_iota(jnp.int32, (C, C), 1)
        causal = li >= mi
        diff = jnp.where(causal, (li - mi).astype(F32), 0.0)
        row = lax.broadcasted_iota(jnp.int32, (C, RET_DK), 0).astype(F32)
        for h in range(RET_HEADS):
            dm_scr[h] = jnp.where(causal, jnp.exp(diff * _LOG_G[h]), 0.0)
            rd_scr[h] = jnp.exp((row + 1.0) * _LOG_G[h])
            kd_scr[h] = jnp.exp((C - 1.0 - row) * _LOG_G[h])

    @pl.when(c == 0)
    def _():
        if has_init:
            s_ref[...] = s0_ref[...]
        else:
            s_ref[...] = jnp.zeros(s_ref.shape, F32)

    for h in range(RET_HEADS):
        sl = slice(h * RET_DK, (h + 1) * RET_DK)
        q = q_ref[:, sl]
        k = k_ref[:, sl]
        v = v_ref[:, sl]
        state = s_ref[0, h]
        sc = lax.dot_general(q, k, (((1,), (1,)), ((), ())), preferred_element_type=F32)
        sc = sc * dm_scr[h]
        intra = jnp.dot(sc.astype(BF16), v, preferred_element_type=F32)
        inter = jnp.dot(q, state.astype(BF16), preferred_element_type=F32) * rd_scr[h]
        o = intra + inter
        kw = (k.astype(F32) * kd_scr[h]).astype(BF16)
        upd = lax.dot_general(kw, v, (((0,), (0,)), ((), ())), preferred_element_type=F32)
        s_ref[0, h] = state * math.exp(C * _LOG_G[h]) + upd
        ms = jnp.mean(o * o, axis=-1, keepdims=True)
        z = z_ref[:, sl].astype(F32)
        y = o * lax.rsqrt(ms + NORM_EPS) * gn_ref[:, sl] * (z * jax.nn.sigmoid(z))
        y_ref[:, sl] = y.astype(BF16)


def _retention(p, ret_gn_row, n_seq, seq_len, C, init_state):
    n_chunks = seq_len // C
    has_init = init_state is not None

    def col(t):
        return pl.BlockSpec((C, COL_W), lambda b, c: (b * n_chunks + c, t))

    in_specs = [col(COL_QR), col(COL_KR), col(COL_VR), col(COL_ZR),
                pl.BlockSpec((1, COL_W), lambda b, c: (0, 0))]
    args = [p, p, p, p, ret_gn_row]
    state_spec = pl.BlockSpec((1, RET_HEADS, RET_DK, RET_DV), lambda b, c: (b, 0, 0, 0))
    if has_init:
        in_specs.append(state_spec)
        args.append(init_state)
    return pl.pallas_call(
        functools.partial(_ret_kernel, C=C, has_init=has_init),
        grid=(n_seq, n_chunks), in_specs=in_specs,
        out_specs=[pl.BlockSpec((C, COL_W), lambda b, c: (b * n_chunks + c, 0)), state_spec],
        out_shape=[jax.ShapeDtypeStruct((n_seq * seq_len, COL_W), BF16),
                   jax.ShapeDtypeStruct((n_seq, RET_HEADS, RET_DK, RET_DV), F32)],
        scratch_shapes=[pltpu.VMEM((RET_HEADS, C, C), F32),
                        pltpu.VMEM((RET_HEADS, C, RET_DK), F32),
                        pltpu.VMEM((RET_HEADS, C, RET_DK), F32)],
        compiler_params=_cparams(2), name="retention_init" if has_init else "retention",
    )(*args)


def _lambda_value(lq1, lk1, lq2, lk2, lam_init):
    a = jnp.sum(lq1[...] * lk1[...], axis=-1, keepdims=True)
    b = jnp.sum(lq2[...] * lk2[...], axis=-1, keepdims=True)
    return jnp.exp(a) - jnp.exp(b) + lam_init


def _bf16_split3(x):
    out, r = [], np.float64(x)
    for _ in range(3):
        c = np.float64(np.asarray(r, np.float32).astype(jnp.bfloat16).astype(np.float32))
        out.append(float(c))
        r = r - c
    return out


def _alibi_operands(blk):
    c = _bf16_split3(LOG2E)
    pos = np.arange(blk)
    ek = np.zeros((blk, 2 * DA_HEAD_DIM), np.float32)
    bq = np.zeros((2 * DA_HEAD_DIM, blk), np.float32)
    for t in range(3):
        ek[:, t] = pos >> 4
        ek[:, 3 + t] = pos & 15
        ek[:, 6 + t] = -16.0 * c[t]
        ek[:, 9 + t] = -c[t]
        bq[t, :] = 16.0 * c[t]
        bq[3 + t, :] = c[t]
        bq[6 + t, :] = pos >> 4
        bq[9 + t, :] = pos & 15
    return jnp.asarray(ek, BF16), jnp.asarray(bq, BF16)


def _attn_kernel(slog2_ref, spow2_ref, qt1_ref, qt2_ref, k1_ref, k2_ref, vt_ref, ek_ref, bq_ref,
                 lq1, lk1, lq2, lk2, gn_ref, o_ref,
                 qx_scr, m_scr, l_scr, acc_scr, s_scr, mb_scr, dbias_scr, *, blk, lam_init):
    j = pl.program_id(1)
    qi = pl.program_id(2)
    half = DA_HEAD_DIM
    hd2 = 2 * DA_HEAD_DIM
    k_refs = (k1_ref, k2_ref)

    zeros = jnp.zeros((half, blk), BF16)
    for hh in range(2):
        bias_rows = (bq_ref[...].astype(F32) * spow2_ref[2 * j + hh]).astype(BF16)
        for c, qref in enumerate((qt1_ref, qt2_ref)):
            idx = 2 * c + hh
            if hh == 0:
                qx_scr[idx, :half, :] = qref[0, :half, :]
                qx_scr[idx, half:hd2, :] = zeros
            else:
                qx_scr[idx, :half, :] = zeros
                qx_scr[idx, half:hd2, :] = qref[0, half:, :]
            qx_scr[idx, hd2:, :] = bias_rows
    m_scr[...] = jnp.full(m_scr.shape, _NEG, F32)
    l_scr[...] = jnp.zeros(l_scr.shape, F32)
    acc_scr[...] = jnp.zeros(acc_scr.shape, F32)

    @pl.when(qi == 0)
    def _():
        kk = lax.broadcasted_iota(jnp.int32, (blk, blk), 0)
        qq = lax.broadcasted_iota(jnp.int32, (blk, blk), 1)
        allowed = (kk // CHUNK) <= (qq // CHUNK)
        dist = jnp.abs(kk - qq).astype(F32)
        for hh in range(2):
            dbias_scr[hh] = jnp.where(allowed, -slog2_ref[2 * j + hh] * dist, _NEG)

    def score_diag(par):
        row0 = pl.multiple_of(qi * blk, blk)
        for c in range(2):
            kc = k_refs[c][pl.ds(row0, blk), :]
            for hh in range(2):
                idx = 2 * c + hh
                s = jnp.dot(kc, qx_scr[idx, :hd2, :], preferred_element_type=F32) + dbias_scr[hh]
                s_scr[par, idx] = s
                mb_scr[par, idx] = jnp.max(s, axis=0, keepdims=True)

    qw = blk // Q_SPLIT
    tiles = [(idx, pl.ds(h * qw, qw)) for idx in range(4) for h in range(Q_SPLIT)]

    def score_tile(ki, par, idx, qs):
        row0 = pl.multiple_of(ki * blk, blk)
        kx = jnp.concatenate([k_refs[idx // 2][pl.ds(row0, blk), :], ek_ref[...]], axis=1)
        s = jnp.dot(kx, qx_scr[idx, :, qs], preferred_element_type=F32)
        s_scr[par, idx, :, qs] = s
        mb_scr[par, idx, :, qs] = jnp.max(s, axis=0, keepdims=True)

    def consume_tile(kb, par, idx, qs):
        hh = idx % 2
        shift = slog2_ref[2 * j + hh] * ((qi - kb) * blk).astype(F32)
        m_old = m_scr[idx, :, qs]
        m_new = jnp.maximum(m_old, mb_scr[par, idx, :, qs] - shift)
        alpha = jnp.exp2(m_old - m_new)
        p = jnp.exp2(s_scr[par, idx, :, qs] - (m_new + shift))
        l_scr[idx, :, qs] = alpha * l_scr[idx, :, qs] + jnp.sum(p, axis=0, keepdims=True)
        vt_h = vt_ref[kb, hh * hd2:(hh + 1) * hd2, :]
        acc_scr[idx, :, qs] = alpha * acc_scr[idx, :, qs] + jnp.dot(
            vt_h, p.astype(BF16), preferred_element_type=F32)
        m_scr[idx, :, qs] = m_new

    def consume(kb, par):
        for idx, qs in tiles:
            consume_tile(kb, par, idx, qs)

    def stage(ki, par, kb):
        for idx, qs in tiles:
            score_tile(ki, par, idx, qs)
            consume_tile(kb, 1 - par, idx, qs)

    score_diag(0)

    def pair_body(u, carry):
        t0 = 2 * u
        stage(t0, 1, jnp.where(t0 == 0, qi, t0 - 1))
        stage(t0 + 1, 0, t0)
        return carry

    lax.fori_loop(0, qi // 2, pair_body, 0)

    @pl.when(qi % 2 == 1)
    def _():
        t0 = qi - 1
        stage(t0, 1, jnp.where(t0 == 0, qi, t0 - 1))
        consume(t0, 1)

    @pl.when(qi % 2 == 0)
    def _():
        consume(jnp.where(qi == 0, qi, qi - 1), 0)

    lam = _lambda_value(lq1, lk1, lq2, lk2, lam_init)
    for hh in range(2):
        o = acc_scr[hh] / l_scr[hh] - lam * (acc_scr[2 + hh] / l_scr[2 + hh])
        ms = jnp.mean(o * o, axis=0, keepdims=True)
        on = (o * lax.rsqrt(ms + NORM_EPS)).T
        sl = slice(hh * hd2, (hh + 1) * hd2)
        o_ref[:, sl] = (on * gn_ref[:, sl] * (1.0 - lam_init)).astype(BF16)


def _attention_prompt(p, qt, vt, lam_rows, da_gn_row, n_seq, seq_len, blk, lam_init):
    nb = seq_len // blk
    n_hp = DA_HEADS // 2
    hd2 = 2 * DA_HEAD_DIM
    kcol0 = COL_KD * COL_W // hd2
    ek, bq = _alibi_operands(blk)

    smem = pl.BlockSpec(memory_space=pltpu.SMEM)
    in_specs = [
        smem, smem,
        pl.BlockSpec((1, hd2, blk), lambda b, j, q: (b * nb + q, j, 0)),
        pl.BlockSpec((1, hd2, blk), lambda b, j, q: (b * nb + q, n_hp + j, 0)),
        pl.BlockSpec((seq_len, hd2), lambda b, j, q: (b, kcol0 + j)),
        pl.BlockSpec((seq_len, hd2), lambda b, j, q: (b, kcol0 + n_hp + j)),
        pl.BlockSpec((nb, 2 * hd2, blk), lambda b, j, q: (b, j, 0)),
        pl.BlockSpec((blk, hd2), lambda b, j, q: (0, 0)),
        pl.BlockSpec((hd2, blk), lambda b, j, q: (0, 0)),
    ] + [pl.BlockSpec((1, DA_HEAD_DIM), lambda b, j, q: (0, 0))] * 4 + [
        pl.BlockSpec((1, 2 * hd2), lambda b, j, q: (0, j)),
    ]
    return pl.pallas_call(
        functools.partial(_attn_kernel, blk=blk, lam_init=lam_init),
        grid=(n_seq, n_hp, nb), in_specs=in_specs,
        out_specs=pl.BlockSpec((blk, 2 * hd2), lambda b, j, q: (b * nb + q, j)),
        out_shape=jax.ShapeDtypeStruct((n_seq * seq_len, COL_W), BF16),
        scratch_shapes=[pltpu.VMEM((4, 2 * hd2, blk), BF16),
                        pltpu.VMEM((4, 1, blk), F32),
                        pltpu.VMEM((4, 1, blk), F32),
                        pltpu.VMEM((4, hd2, blk), F32),
                        pltpu.VMEM((2, 4, blk, blk), F32),
                        pltpu.VMEM((2, 4, 1, blk), F32),
                        pltpu.VMEM((2, blk, blk), F32)],
        compiler_params=_cparams(3), name="diff_attn_prompt",
    )(jnp.asarray(_SLOPES_LOG2), jnp.asarray(_SLOPES), qt, qt, p, p, vt, ek, bq,
      *lam_rows, da_gn_row)


def _sattn_kernel(slopes_ref, q_ref, kn_ref, vn_ref, kc_ref, vc_ref,
                  lq1, lk1, lq2, lk2, gn_ref, o_ref, *, L, P, lam_init):
    lam = _lambda_value(lq1, lk1, lq2, lk2, lam_init)
    hd2 = 2 * DA_HEAD_DIM
    lane = lax.broadcasted_iota(jnp.int32, (L, hd2), 1)
    qpos_c = P + lax.broadcasted_iota(jnp.int32, (L, P), 0)
    kpos_c = lax.broadcasted_iota(jnp.int32, (L, P), 1)
    qpos_n = P + lax.broadcasted_iota(jnp.int32, (L, L), 0)
    kpos_n = P + lax.broadcasted_iota(jnp.int32, (L, L), 1)
    dist_c = jnp.abs(qpos_c - kpos_c).astype(F32)
    dist_n = jnp.abs(qpos_n - kpos_n).astype(F32)
    all_allowed = P % CHUNK == 0 and L <= CHUNK
    ok_c = (kpos_c // CHUNK) <= (qpos_c // CHUNK)
    ok_n = (kpos_n // CHUNK) <= (qpos_n // CHUNK)
    nt = (((1,), (1,)), ((), ()))
    for pr in range(DA_HEADS // 2):
        sls = [slice((c * (DA_HEADS // 2) + pr) * hd2, (c * (DA_HEADS // 2) + pr + 1) * hd2)
               for c in range(2)]
        q128s = [q_ref[:, sl].astype(F32) * _Q_SCALE_LOG2 for sl in sls]
        kc_ts = [kc_ref[0, sl, :].astype(BF16) for sl in sls]
        for hh in range(2):
            h = 2 * pr + hh
            slope = slopes_ref[h]
            probs = []
            for c in range(2):
                keep = (lane < DA_HEAD_DIM) if hh == 0 else (lane >= DA_HEAD_DIM)
                qm = jnp.where(keep, q128s[c], 0.0).astype(BF16)
                s_c = jnp.dot(qm, kc_ts[c], preferred_element_type=F32) - slope * dist_c
                s_n = lax.dot_general(qm, kn_ref[:, sls[c]], nt,
                                      preferred_element_type=F32) - slope * dist_n
                if not all_allowed:
                    s_c = jnp.where(ok_c, s_c, _NEG)
                    s_n = jnp.where(ok_n, s_n, _NEG)
                m = jnp.maximum(jnp.max(s_c, axis=-1, keepdims=True),
                                jnp.max(s_n, axis=-1, keepdims=True))
                p_c = jnp.exp2(s_c - m)
                p_n = jnp.exp2(s_n - m)
                inv = 1.0 / (jnp.sum(p_c, axis=-1, keepdims=True)
                             + jnp.sum(p_n, axis=-1, keepdims=True))
                probs.append((p_c * inv, p_n * inv))
            a_c = (probs[0][0] - lam * probs[1][0]).astype(BF16)
            a_n = (probs[0][1] - lam * probs[1][1]).astype(BF16)
            vsl = slice(h * hd2, (h + 1) * hd2)
            vc = vc_ref[0, pl.ds(h, P, stride=DA_HEADS), :].astype(BF16)
            o = (jnp.dot(a_c, vc, preferred_element_type=F32)
                 + jnp.dot(a_n, vn_ref[:, vsl], preferred_element_type=F32))
            ms = jnp.mean(o * o, axis=-1, keepdims=True)
            on = o * lax.rsqrt(ms + NORM_EPS) * gn_ref[:, vsl] * (1.0 - lam_init)
            o_ref[:, vsl] = on.astype(BF16)


def _attention_sample(p, cache_kt, cache_v, lam_rows, da_gn_row, n_seq, L, lam_init):
    P = cache_kt.shape[2]
    smem = pl.BlockSpec(memory_space=pltpu.SMEM)
    in_specs = [
        smem,
        pl.BlockSpec((L, COL_W), lambda s: (s, COL_QD)),
        pl.BlockSpec((L, COL_W), lambda s: (s, COL_KD)),
        pl.BlockSpec((L, COL_W), lambda s: (s, COL_VD)),
        pl.BlockSpec((1, COL_W, P), lambda s: (s, 0, 0)),
        pl.BlockSpec((1, P * DA_HEADS, RET_DV), lambda s: (s, 0, 0)),
    ] + [pl.BlockSpec((1, DA_HEAD_DIM), lambda s: (0, 0))] * 4 + [
        pl.BlockSpec((1, COL_W), lambda s: (0, 0)),
    ]
    return pl.pallas_call(
        functools.partial(_sattn_kernel, L=L, P=P, lam_init=lam_init),
        grid=(n_seq,), in_specs=in_specs,
        out_specs=pl.BlockSpec((L, COL_W), lambda s: (s, 0)),
        out_shape=jax.ShapeDtypeStruct((n_seq * L, COL_W), BF16),
        compiler_params=_cparams(1), name="diff_attn_sample",
    )(jnp.asarray(_SLOPES_LOG2), p, p, p, cache_kt, cache_v, *lam_rows, da_gn_row)


def _merge_kernel(x_ref, yr_ref, yd_ref, gr0, gr1, gd0, gd1, wr_ref, wd_ref, wo_ref, gm_ref,
                  x1_ref, h2_ref):
    yr = jnp.dot(yr_ref[...], wr_ref[...], preferred_element_type=F32)
    yd = jnp.dot(yd_ref[...], wd_ref[...], preferred_element_type=F32)
    gr = jnp.concatenate([gr0[...], gr1[...]], axis=1).astype(F32)
    gd = jnp.concatenate([gd0[...], gd1[...]], axis=1).astype(F32)
    m = jax.nn.sigmoid(gr) * yr + jax.nn.sigmoid(gd) * yd
    x1 = x_ref[...] + jnp.dot(m.astype(BF16), wo_ref[...], preferred_element_type=F32)
    x1_ref[...] = x1
    ms = jnp.mean(x1 * x1, axis=-1, keepdims=True)
    h2_ref[...] = (x1 * lax.rsqrt(ms + NORM_EPS) * gm_ref[...]).astype(BF16)


def _merge(x2d, yr_in, yd_in, p, w_ret_o, w_da_o, w_out, g_mlp_row, tm):
    T = x2d.shape[0]

    def whole(shape):
        return pl.BlockSpec(shape, lambda i: (0, 0), pipeline_mode=pl.Buffered(1))

    def pcol(t):
        return pl.BlockSpec((tm, COL_W), lambda i: (i, t))

    in_specs = [
        pl.BlockSpec((tm, D_MODEL), lambda i: (i, 0)),
        pl.BlockSpec((tm, COL_W), lambda i: (i, 0)),
        pl.BlockSpec((tm, COL_W), lambda i: (i, 0)),
        pcol(COL_GR), pcol(COL_GR + 1), pcol(COL_GD), pcol(COL_GD + 1),
        whole((COL_W, D_MODEL)), whole((COL_W, D_MODEL)), whole((D_MODEL, D_MODEL)),
        pl.BlockSpec((1, D_MODEL), lambda i: (0, 0)),
    ]
    return pl.pallas_call(
        _merge_kernel, grid=(T // tm,), in_specs=in_specs,
        out_specs=[pl.BlockSpec((tm, D_MODEL), lambda i: (i, 0)),
                   pl.BlockSpec((tm, D_MODEL), lambda i: (i, 0))],
        out_shape=[jax.ShapeDtypeStruct((T, D_MODEL), F32),
                   jax.ShapeDtypeStruct((T, D_MODEL), BF16)],
        compiler_params=_cparams(1), name="merge",
    )(x2d, yr_in, yd_in, p, p, p, p, w_ret_o, w_da_o, w_out, g_mlp_row)


def _mlp_kernel(x1_ref, h2_ref, wu_ref, wd_ref, y_ref, r_scr, *, n_steps, nf):
    t = pl.program_id(0)
    fb = (jnp.maximum(t, 1) - 1) % nf

    def up(par):
        u = jnp.dot(h2_ref[...], wu_ref[...], preferred_element_type=F32)
        r_scr[par] = jnp.square(jnp.maximum(u, 0.0)).astype(BF16)

    def down(par, first_chunk):
        contrib = jnp.dot(r_scr[par], wd_ref[...], preferred_element_type=F32)
        if first_chunk:
            y_ref[...] = x1_ref[...] + contrib
        else:
            y_ref[...] += contrib

    mid = (t > 0) & (t < n_steps - 1)

    @pl.when(t == 0)
    def _():
        up(0)

    @pl.when(mid & (t % 2 == 0))
    def _():
        down(1, False)
        up(0)

    @pl.when(mid & (t % 2 == 1) & (fb == 0))
    def _():
        down(0, True)
        up(1)

    @pl.when(mid & (t % 2 == 1) & (fb != 0))
    def _():
        down(0, False)
        up(1)

    @pl.when(t == n_steps - 1)
    def _():
        down((n_steps - 2) % 2, False)


def _mlp(x1, h2, w_up, w_down, tm, tf):
    T = x1.shape[0]
    nf = D_FF // tf
    assert nf % 2 == 0 and nf >= 2
    n_up = (T // tm) * nf
    n_steps = n_up + 1

    def up_idx(t):
        return jnp.minimum(t, n_up - 1)

    def down_idx(t):
        return jnp.maximum(t, 1) - 1

    return pl.pallas_call(
        functools.partial(_mlp_kernel, n_steps=n_steps, nf=nf), grid=(n_steps,),
        in_specs=[pl.BlockSpec((tm, D_MODEL), lambda t: (down_idx(t) // nf, 0)),
                  pl.BlockSpec((tm, D_MODEL), lambda t: (up_idx(t) // nf, 0)),
                  pl.BlockSpec((D_MODEL, tf), lambda t: (0, up_idx(t) % nf)),
                  pl.BlockSpec((tf, D_MODEL), lambda t: (down_idx(t) % nf, 0))],
        out_specs=pl.BlockSpec((tm, D_MODEL), lambda t: (down_idx(t) // nf, 0)),
        out_shape=jax.ShapeDtypeStruct((T, D_MODEL), F32),
        scratch_shapes=[pltpu.VMEM((2, tm, tf), BF16)],
        compiler_params=_cparams(1), name="mlp",
    )(x1, h2, w_up, w_down)


def _block_diag_mean():
    r = np.arange(V7X_MXU_DIM) // DA_HEAD_DIM
    return jnp.asarray((r[:, None] == r[None, :]).astype(np.float32) / DA_HEAD_DIM, BF16)


def kernel(x_prompt, x_sample, cache_da_k, cache_da_v, state_ret, g_mix, w_in, ret_gn, qn_g, kn_g,
           lambda_q1, lambda_k1, lambda_q2, lambda_k2, da_gn, w_ret_o, w_da_o, w_out, g_mlp,
           w_up, w_down):
    B, S, _ = x_prompt.shape
    DB, L, _ = x_sample.shape
    P = cache_da_k.shape[2]
    depth = w_in.shape[0]
    assert S % TOKEN_TILE == 0 and S % RET_CHUNK_PROMPT == 0 and L % 16 == 0

    tm_p = TOKEN_TILE
    tm_s = _pick_tile(DB * L, TOKEN_TILE)
    tm_ip = _pick_tile(S, INPROJ_TILE)
    tm_is = _pick_tile(DB * L, INPROJ_TILE)
    assert tm_ip % TOKEN_TILE == 0 and (tm_is % L == 0 or L % tm_is == 0)
    rope_p = _rope_operands(np.arange(S // tm_ip) * tm_ip, S // tm_ip, np.arange(tm_ip))
    if tm_is >= L:
        rope_s = _rope_operands([0], 1, P + np.arange(tm_is) % L)
    else:
        rope_s = _rope_operands(P + np.arange(L // tm_is) * tm_is, L // tm_is, np.arange(tm_is))
    bd = _block_diag_mean()
    n_heads_qk = COL_W // DA_HEAD_DIM

    xp = x_prompt.reshape(B * S, D_MODEL)
    xs = x_sample.reshape(DB * L, D_MODEL)
    kp_l, vp_l, rp_l, ks_l, vs_l, rs_l = [], [], [], [], [], []
    for l in range(depth):
        lam_init = 0.8 - 0.6 * math.exp(-0.3 * l)
        w_in_bf = w_in[l].astype(BF16)
        w_ro, w_do, w_o = w_ret_o[l].astype(BF16), w_da_o[l].astype(BF16), w_out[l].astype(BF16)
        w_u, w_d = w_up[l].astype(BF16), w_down[l].astype(BF16)
        g_mix_row = g_mix[l].reshape(1, D_MODEL)
        g_mlp_row = g_mlp[l].reshape(1, D_MODEL)
        qg16 = jnp.tile(qn_g[l], n_heads_qk).reshape(1, COL_W)
        kg16 = jnp.tile(kn_g[l], n_heads_qk).reshape(1, COL_W)
        ret_gn_row = ret_gn[l].reshape(1, COL_W)
        da_gn_row = da_gn[l].reshape(1, COL_W)
        lam_rows = [a[l].astype(F32).reshape(1, DA_HEAD_DIM)
                    for a in (lambda_q1, lambda_k1, lambda_q2, lambda_k2)]

        h, p = _inproj0(xp, g_mix_row, w_in_bf, rope_p, tm_ip)
        p, nkt, nv, qt, vt = _inproj(h, p, w_in_bf, rope_p, qg16, kg16, bd, tm_ip, S)
        yr_in, s_fin = _retention(p, ret_gn_row, B, S, RET_CHUNK_PROMPT, None)
        yd_in = _attention_prompt(p, qt, vt, lam_rows, da_gn_row, B, S, TOKEN_TILE, lam_init)
        x1, h2 = _merge(xp, yr_in, yd_in, p, w_ro, w_do, w_o, g_mlp_row, tm_p)
        xp = _mlp(x1, h2, w_u, w_d, tm_p, MLP_FF_TILE)
        nk = jnp.transpose(nkt.reshape(B, 2 * DA_HEADS, DA_HEAD_DIM, S), (0, 3, 1, 2))
        kp_l.append(nk)
        vp_l.append(nv.reshape(B, S, DA_HEADS, 2 * DA_HEAD_DIM))
        rp_l.append(s_fin)

        h, p = _inproj0(xs, g_mix_row, w_in_bf, rope_s, tm_is)
        p, nk, nv = _inproj(h, p, w_in_bf, rope_s, qg16, kg16, bd, tm_is, None)
        yr_in, s_new = _retention(p, ret_gn_row, DB, L, L, state_ret[l].astype(F32))
        cache_kt = jnp.transpose(cache_da_k[l].reshape(DB, P, COL_W), (0, 2, 1))
        yd_in = _attention_sample(p, cache_kt, cache_da_v[l].reshape(DB, P * DA_HEADS, RET_DV),
                                  lam_rows, da_gn_row, DB, L, lam_init)
        x1, h2 = _merge(xs, yr_in, yd_in, p, w_ro, w_do, w_o, g_mlp_row, tm_s)
        xs = _mlp(x1, h2, w_u, w_d, tm_s, MLP_FF_TILE)
        ks_l.append(nk.reshape(DB, L, 2 * DA_HEADS, DA_HEAD_DIM))
        vs_l.append(nv.reshape(DB, L, DA_HEADS, 2 * DA_HEAD_DIM))
        rs_l.append(s_new)

    return (xp.reshape(B, S, D_MODEL), xs.reshape(DB, L, D_MODEL),
            jnp.stack(kp_l, 0), jnp.stack(vp_l, 0), jnp.stack(rp_l, 0),
            jnp.stack(ks_l, 0), jnp.stack(vs_l, 0), jnp.stack(rs_l, 0))
```

```python
import functools
import math

import numpy as np
import jax
import jax.numpy as jnp
from jax import lax
from jax.experimental import pallas as pl
from jax.experimental.pallas import tpu as pltpu

D_MODEL = 2048
CHUNK = 64
RET_HEADS = 8
RET_DK = 128
RET_DV = 128
DA_HEADS = 8
DA_HEAD_DIM = 64
D_FF = 4 * D_MODEL
ROPE_BASE = 10000.0
NORM_EPS = 1e-6
LOG2E = 1.4426950408889634

V7X_MXU_DIM = 256
V7X_VMEM_LIMIT_BYTES = 56 * 1024 * 1024

COL_W = 1024
N_COL_TILES = 11
COL_QR, COL_KR, COL_VR, COL_ZR, COL_QD, COL_KD, COL_VD = 0, 1, 2, 3, 4, 5, 6
COL_GR, COL_GD = 7, 9

TOKEN_TILE = 512
INPROJ_TILE = 1024
RET_CHUNK_PROMPT = 256
MLP_FF_TILE = 1024
Q_SPLIT = 2

F32 = jnp.float32
BF16 = jnp.bfloat16

_LOG_G = [math.log1p(-(2.0 ** (-5.0 - h))) for h in range(RET_HEADS)]
_SLOPES = np.asarray([2.0 ** (-8.0 * (h + 1) / DA_HEADS) for h in range(DA_HEADS)], np.float32)
_SLOPES_LOG2 = (_SLOPES * np.float32(LOG2E)).astype(np.float32)
_Q_SCALE_LOG2 = (DA_HEAD_DIM ** -0.5) * LOG2E
_NEG = -1e30


def _cparams(n_grid):
    return pltpu.CompilerParams(
        dimension_semantics=("arbitrary",) * n_grid,
        vmem_limit_bytes=V7X_VMEM_LIMIT_BYTES)


def _pick_tile(n, cap):
    t = cap
    while n % t:
        t //= 2
    return t


def _inproj0_kernel(x_ref, g_ref, w_ref, ca_ref, sa_ref, sas_ref, cb_ref, sb_ref, sbs_ref,
                    h_ref, p_ref, s_scr, *, n_tok):
    t = pl.program_id(0)
    n_part = 4
    rows = x_ref.shape[0] // n_part

    def norm_matmul():
        for c in range(n_part):
            rs = pl.ds(c * rows, rows)
            x = x_ref[rs, :]
            ms = jnp.mean(x * x, axis=-1, keepdims=True)
            h = (x * lax.rsqrt(ms + NORM_EPS) * g_ref[...]).astype(BF16)
            h_ref[rs, :] = h
            s_scr[rs, :] = jnp.dot(h, w_ref[...], preferred_element_type=F32)

    def rope():
        ca, sa, sas = ca_ref[0], sa_ref[0], sas_ref[0]
        cos = ca * cb_ref[...] - sa * sb_ref[...]
        sin = sas * cb_ref[...] + ca * sbs_ref[...]
        for hh in range(RET_HEADS):
            sl = slice(hh * RET_DK, (hh + 1) * RET_DK)
            a = s_scr[:, sl]
            p_ref[:, sl] = (a * cos + pltpu.roll(a, RET_DK // 2, axis=1) * sin).astype(BF16)

    @pl.when(t == 0)
    def _():
        norm_matmul()

    @pl.when((t > 0) & (t < n_tok))
    def _():
        rope()
        norm_matmul()

    @pl.when(t == n_tok)
    def _():
        rope()


def _inproj0(x2d, g_row, w_bf, rope_ops, tm):
    T = x2d.shape[0]
    n_tok = T // tm
    n_rope = rope_ops[0].shape[0]
    last = n_tok - 1

    def ep(t):
        return jnp.maximum(t, 1) - 1

    base_spec = pl.BlockSpec((1, 1, RET_DK), lambda t: (ep(t) % n_rope, 0, 0))
    row_spec = pl.BlockSpec((tm, RET_DK), lambda t: (0, 0))
    return pl.pallas_call(
        functools.partial(_inproj0_kernel, n_tok=n_tok), grid=(n_tok + 1,),
        in_specs=[pl.BlockSpec((tm, D_MODEL), lambda t: (jnp.minimum(t, last), 0)),
                  pl.BlockSpec((1, D_MODEL), lambda t: (0, 0)),
                  pl.BlockSpec((D_MODEL, COL_W), lambda t: (0, COL_QR)),
                  base_spec, base_spec, base_spec, row_spec, row_spec, row_spec],
        out_specs=[pl.BlockSpec((tm, D_MODEL), lambda t: (jnp.minimum(t, last), 0)),
                   pl.BlockSpec((tm, COL_W), lambda t: (ep(t), COL_QR))],
        out_shape=[jax.ShapeDtypeStruct((T, D_MODEL), BF16),
                   jax.ShapeDtypeStruct((T, N_COL_TILES * COL_W), BF16)],
        scratch_shapes=[pltpu.VMEM((tm, COL_W), F32)],
        compiler_params=_cparams(1), name="inproj_first",
    )(x2d, g_row, w_bf, *rope_ops)


def _inproj_kernel(h_ref, w_ref, ca_ref, sa_ref, sas_ref, cb_ref, sb_ref, sbs_ref,
                   qg_ref, kg_ref, bd_ref, p_in_ref,
                   p_ref, nk_ref, nv_ref, *rest, emit_t, n_steps, n_tok):
    if emit_t:
        qt_ref, vt_ref, s_scr = rest
    else:
        (s_scr,) = rest
    del p_in_ref
    t = pl.program_id(0)
    jb = COL_KR + (jnp.maximum(t, 1) - 1) // n_tok

    def matmul():
        s_scr[...] = jnp.dot(h_ref[...], w_ref[...], preferred_element_type=F32)

    def rope(scale):
        ca, sa, sas = ca_ref[0], sa_ref[0], sas_ref[0]
        cos = ca * cb_ref[...] - sa * sb_ref[...]
        sin = sas * cb_ref[...] + ca * sbs_ref[...]
        for hh in range(RET_HEADS):
            sl = slice(hh * RET_DK, (hh + 1) * RET_DK)
            a = s_scr[:, sl]
            o = a * cos + pltpu.roll(a, RET_DK // 2, axis=1) * sin
            if scale != 1.0:
                o = o * scale
            p_ref[:, sl] = o.astype(BF16)

    def plain():
        p_ref[...] = s_scr[...].astype(BF16)

    def store_feature_major(ref, val_t):
        for b in range(ref.shape[0]):
            ref[b] = val_t[:, b * TOKEN_TILE:(b + 1) * TOKEN_TILE]

    def head_norm(acc, gain_row):
        sq = (acc * acc).astype(BF16)
        parts = [
            jnp.dot(sq[:, c * V7X_MXU_DIM:(c + 1) * V7X_MXU_DIM], bd_ref[...],
                    preferred_element_type=F32)
            for c in range(COL_W // V7X_MXU_DIM)
        ]
        ms = jnp.concatenate(parts, axis=1)
        return acc * lax.rsqrt(ms + NORM_EPS) * gain_row

    def q_diff():
        n = head_norm(s_scr[...], qg_ref[...])
        p_ref[...] = n.astype(BF16)
        if emit_t:
            store_feature_major(qt_ref, (n * _Q_SCALE_LOG2).T.astype(BF16))

    def k_diff():
        n = head_norm(s_scr[...], kg_ref[...])
        if emit_t:
            nk_ref[0] = n.T
        else:
            nk_ref[...] = n
        p_ref[...] = n.astype(BF16)

    def v_diff():
        acc = s_scr[...]
        for h in range(DA_HEADS):
            nv_ref[pl.ds(h, acc.shape[0], stride=DA_HEADS), :] = acc[:, h * RET_DV:(h + 1) * RET_DV]
        p_ref[...] = acc.astype(BF16)
        if emit_t:
            store_feature_major(vt_ref, acc.T.astype(BF16))

    epilogues = {COL_KR: lambda: rope(RET_DK ** -0.5), COL_QD: q_diff, COL_KD: k_diff,
                 COL_VD: v_diff}
    mid = (t > 0) & (t < n_steps - 1)

    @pl.when(t == 0)
    def _():
        matmul()

    for col, epilogue in epilogues.items():
        @pl.when(mid & (jb == col))
        def _(epilogue=epilogue):
            epilogue()
            matmul()

    @pl.when(mid & ((jb == COL_VR) | (jb == COL_ZR) | (jb >= COL_GR)))
    def _():
        plain()
        matmul()

    @pl.when(t == n_steps - 1)
    def _():
        plain()


def _rope_operands(base_pos, n_base, row_pos):
    half = RET_DK // 2
    inv = np.exp(-math.log(ROPE_BASE) * np.arange(half, dtype=np.float64) / half)
    sign = np.concatenate([-np.ones(half), np.ones(half)])

    def tables(pos):
        ang = np.asarray(pos, np.float64)[:, None] * inv[None, :]
        cos = np.concatenate([np.cos(ang)] * 2, axis=1)
        sin = np.concatenate([np.sin(ang)] * 2, axis=1)
        return cos, sin, sin * sign

    out_a = [jnp.asarray(x.reshape(n_base, 1, RET_DK), F32) for x in tables(base_pos)]
    out_b = [jnp.asarray(x, F32) for x in tables(row_pos)]
    return out_a + out_b


def _inproj(h2d, p_first, w_bf, rope_ops, qg16, kg16, bd, tm, seq_len):
    emit_t = seq_len is not None
    T = h2d.shape[0]
    n_tok = T // tm
    n_rope = rope_ops[0].shape[0]
    n_mm = n_tok * (N_COL_TILES - COL_KR)
    n_steps = n_mm + 1

    def mm_tile(t):
        tt = jnp.minimum(t, n_mm - 1)
        return COL_KR + tt // n_tok, tt % n_tok

    def ep_tile(t):
        tt = jnp.maximum(t, 1) - 1
        return COL_KR + tt // n_tok, tt % n_tok

    def side_tile(t, col):
        j, i = ep_tile(t)
        return jnp.where(j < col, 0, jnp.where(j > col, n_tok - 1, i))

    base_spec = pl.BlockSpec((1, 1, RET_DK), lambda t: (ep_tile(t)[1] % n_rope, 0, 0))
    row_spec = pl.BlockSpec((tm, RET_DK), lambda t: (0, 0))
    in_specs = [
        pl.BlockSpec((tm, D_MODEL), lambda t: (mm_tile(t)[1], 0)),
        pl.BlockSpec((D_MODEL, COL_W), lambda t: (0, mm_tile(t)[0])),
        base_spec, base_spec, base_spec, row_spec, row_spec, row_spec,
        pl.BlockSpec((1, COL_W), lambda t: (0, 0)),
        pl.BlockSpec((1, COL_W), lambda t: (0, 0)),
        pl.BlockSpec((V7X_MXU_DIM, V7X_MXU_DIM), lambda t: (0, 0)),
        pl.BlockSpec(memory_space=pl.ANY),
    ]
    if emit_t:
        spb = seq_len // tm
        nk_shape = jax.ShapeDtypeStruct((T // seq_len, COL_W, seq_len), F32)
        nk_spec = pl.BlockSpec((1, COL_W, tm), lambda t: (side_tile(t, COL_KD) // spb, 0,
                                                          side_tile(t, COL_KD) % spb))
    else:
        nk_shape = jax.ShapeDtypeStruct((T, COL_W), F32)
        nk_spec = pl.BlockSpec((tm, COL_W), lambda t: (side_tile(t, COL_KD), 0))
    out_shape = [
        jax.ShapeDtypeStruct((T, N_COL_TILES * COL_W), BF16),
        nk_shape,
        jax.ShapeDtypeStruct((T * DA_HEADS, RET_DV), F32),
    ]
    out_specs = [
        pl.BlockSpec((tm, COL_W), lambda t: (ep_tile(t)[1], ep_tile(t)[0])),
        nk_spec,
        pl.BlockSpec((tm * DA_HEADS, RET_DV), lambda t: (side_tile(t, COL_VD), 0)),
    ]
    if emit_t:
        ab = tm // TOKEN_TILE
        out_shape += [jax.ShapeDtypeStruct((T // TOKEN_TILE, COL_W, TOKEN_TILE), BF16)] * 2
        out_specs += [
            pl.BlockSpec((ab, COL_W, TOKEN_TILE), lambda t: (side_tile(t, COL_QD), 0, 0)),
            pl.BlockSpec((ab, COL_W, TOKEN_TILE), lambda t: (side_tile(t, COL_VD), 0, 0))]
    return pl.pallas_call(
        functools.partial(_inproj_kernel, emit_t=emit_t, n_steps=n_steps, n_tok=n_tok),
        grid=(n_steps,), in_specs=in_specs, out_specs=out_specs, out_shape=out_shape,
        scratch_shapes=[pltpu.VMEM((tm, COL_W), F32)],
        input_output_aliases={11: 0},
        compiler_params=_cparams(1), name="inproj_t" if emit_t else "inproj",
    )(h2d, w_bf, *rope_ops, qg16, kg16, bd, p_first)


def _ret_kernel(q_ref, k_ref, v_ref, z_ref, gn_ref, *rest, C, has_init):
    if has_init:
        s0_ref, y_ref, s_ref, dm_scr, rd_scr, kd_scr = rest
    else:
        y_ref, s_ref, dm_scr, rd_scr, kd_scr = rest
    b = pl.program_id(0)
    c = pl.program_id(1)

    @pl.when((b == 0) & (c == 0))
    def _():
        li = lax.broadcasted_iota(jnp.int32, (C, C), 0)
        mi = lax.broadcasted_iota(jnp.int32, (C, C), 1)
        causal = li >= mi
        diff = jnp.where(causal, (li - mi).astype(F32), 0.0)
        row = lax.broadcasted_iota(jnp.int32, (C, RET_DK), 0).astype(F32)
        for h in range(RET_HEADS):
            dm_scr[h] = jnp.where(causal, jnp.exp(diff * _LOG_G[h]), 0.0)
            rd_scr[h] = jnp.exp((row + 1.0) * _LOG_G[h])
            kd_scr[h] = jnp.exp((C - 1.0 - row) * _LOG_G[h])

    @pl.when(c == 0)
    def _():
        if has_init:
            s_ref[...] = s0_ref[...]
        else:
            s_ref[...] = jnp.zeros(s_ref.shape, F32)

    def sl_of(h):
        return slice(h * RET_DK, (h + 1) * RET_DK)

    def scores(h):
        return lax.dot_general(q_ref[:, sl_of(h)], k_ref[:, sl_of(h)], (((1,), (1,)), ((), ())),
                               preferred_element_type=F32)

    sc_next = scores(0)
    for h in range(RET_HEADS):
        sl = sl_of(h)
        sc = sc_next
        if h + 1 < RET_HEADS:
            sc_next = scores(h + 1)
        q = q_ref[:, sl]
        k = k_ref[:, sl]
        v = v_ref[:, sl]
        state = s_ref[0, h]
        sc = sc * dm_scr[h]
        intra = jnp.dot(sc.astype(BF16), v, preferred_element_type=F32)
        inter = jnp.dot(q, state.astype(BF16), preferred_element_type=F32) * rd_scr[h]
        o = intra + inter
        kw = (k.astype(F32) * kd_scr[h]).astype(BF16)
        upd = lax.dot_general(kw, v, (((0,), (0,)), ((), ())), preferred_element_type=F32)
        s_ref[0, h] = state * math.exp(C * _LOG_G[h]) + upd
        ms = jnp.mean(o * o, axis=-1, keepdims=True)
        z = z_ref[:, sl].astype(F32)
        y = o * lax.rsqrt(ms + NORM_EPS) * gn_ref[:, sl] * (z * jax.nn.sigmoid(z))
        y_ref[:, sl] = y.astype(BF16)


def _retention(p, ret_gn_row, n_seq, seq_len, C, init_state):
    n_chunks = seq_len // C
    has_init = init_state is not None

    def col(t):
        return pl.BlockSpec((C, COL_W), lambda b, c: (b * n_chunks + c, t))

    in_specs = [col(COL_QR), col(COL_KR), col(COL_VR), col(COL_ZR),
                pl.BlockSpec((1, COL_W), lambda b, c: (0, 0))]
    args = [p, p, p, p, ret_gn_row]
    state_spec = pl.BlockSpec((1, RET_HEADS, RET_DK, RET_DV), lambda b, c: (b, 0, 0, 0))
    if has_init:
        in_specs.append(state_spec)
        args.append(init_state)
    return pl.pallas_call(
        functools.partial(_ret_kernel, C=C, has_init=has_init),
        grid=(n_seq, n_chunks), in_specs=in_specs,
        out_specs=[pl.BlockSpec((C, COL_W), lambda b, c: (b * n_chunks + c, 0)), state_spec],
        out_shape=[jax.ShapeDtypeStruct((n_seq * seq_len, COL_W), BF16),
                   jax.ShapeDtypeStruct((n_seq, RET_HEADS, RET_DK, RET_DV), F32)],
        scratch_shapes=[pltpu.VMEM((RET_HEADS, C, C), F32),
                        pltpu.VMEM((RET_HEADS, C, RET_DK), F32),
                        pltpu.VMEM((RET_HEADS, C, RET_DK), F32)],
        compiler_params=_cparams(2), name="retention_init" if has_init else "retention",
    )(*args)


def _lambda_value(lq1, lk1, lq2, lk2, lam_init):
    a = jnp.sum(lq1[...] * lk1[...], axis=-1, keepdims=True)
    b = jnp.sum(lq2[...] * lk2[...], axis=-1, keepdims=True)
    return jnp.exp(a) - jnp.exp(b) + lam_init


def _bf16_split3(x):
    out, r = [], np.float64(x)
    for _ in range(3):
        c = np.float64(np.asarray(r, np.float32).astype(jnp.bfloat16).astype(np.float32))
        out.append(float(c))
        r = r - c
    return out


def _alibi_operands(blk):
    c = _bf16_split3(LOG2E)
    pos = np.arange(blk)
    ek = np.zeros((blk, 2 * DA_HEAD_DIM), np.float32)
    bq = np.zeros((2 * DA_HEAD_DIM, blk), np.float32)
    for t in range(3):
        ek[:, t] = pos >> 4
        ek[:, 3 + t] = pos & 15
        ek[:, 6 + t] = -16.0 * c[t]
        ek[:, 9 + t] = -c[t]
        bq[t, :] = 16.0 * c[t]
        bq[3 + t, :] = c[t]
        bq[6 + t, :] = pos >> 4
        bq[9 + t, :] = pos & 15
    return jnp.asarray(ek, BF16), jnp.asarray(bq, BF16)


def _attn_kernel(slog2_ref, spow2_ref, qt1_ref, qt2_ref, k1_ref, k2_ref, vt_ref, ek_ref, bq_ref,
                 lq1, lk1, lq2, lk2, gn_ref, o_ref,
                 qx_scr, m_scr, l_scr, acc_scr, s_scr, mb_scr, dbias_scr, *, blk, lam_init):
    j = pl.program_id(1)
    qi = pl.program_id(2)
    half = DA_HEAD_DIM
    hd2 = 2 * DA_HEAD_DIM
    k_refs = (k1_ref, k2_ref)

    zeros = jnp.zeros((half, blk), BF16)
    for hh in range(2):
        bias_rows = (bq_ref[...].astype(F32) * spow2_ref[2 * j + hh]).astype(BF16)
        for c, qref in enumerate((qt1_ref, qt2_ref)):
            idx = 2 * c + hh
            if hh == 0:
                qx_scr[idx, :half, :] = qref[0, :half, :]
                qx_scr[idx, half:hd2, :] = zeros
            else:
                qx_scr[idx, :half, :] = zeros
                qx_scr[idx, half:hd2, :] = qref[0, half:, :]
            qx_scr[idx, hd2:, :] = bias_rows
    m_scr[...] = jnp.full(m_scr.shape, _NEG, F32)
    l_scr[...] = jnp.zeros(l_scr.shape, F32)
    acc_scr[...] = jnp.zeros(acc_scr.shape, F32)

    @pl.when(qi == 0)
    def _():
        kk = lax.broadcasted_iota(jnp.int32, (blk, blk), 0)
        qq = lax.broadcasted_iota(jnp.int32, (blk, blk), 1)
        allowed = (kk // CHUNK) <= (qq // CHUNK)
        dist = jnp.abs(kk - qq).astype(F32)
        for hh in range(2):
            dbias_scr[hh] = jnp.where(allowed, -slog2_ref[2 * j + hh] * dist, _NEG)

    def score_diag(par):
        row0 = pl.multiple_of(qi * blk, blk)
        for c in range(2):
            kc = k_refs[c][pl.ds(row0, blk), :]
            for hh in range(2):
                idx = 2 * c + hh
                s = jnp.dot(kc, qx_scr[idx, :hd2, :], preferred_element_type=F32) + dbias_scr[hh]
                s_scr[par, idx] = s
                mb_scr[par, idx] = jnp.max(s, axis=0, keepdims=True)

    qw = blk // Q_SPLIT
    tiles = [(idx, pl.ds(h * qw, qw)) for idx in range(4) for h in range(Q_SPLIT)]

    def score_tile(ki, par, idx, qs):
        row0 = pl.multiple_of(ki * blk, blk)
        kx = jnp.concatenate([k_refs[idx // 2][pl.ds(row0, blk), :], ek_ref[...]], axis=1)
        s = jnp.dot(kx, qx_scr[idx, :, qs], preferred_element_type=F32)
        s_scr[par, idx, :, qs] = s
        mb_scr[par, idx, :, qs] = jnp.max(s, axis=0, keepdims=True)

    def consume_tile(kb, par, idx, qs):
        hh = idx % 2
        shift = slog2_ref[2 * j + hh] * ((qi - kb) * blk).astype(F32)
        m_old = m_scr[idx, :, qs]
        m_new = jnp.maximum(m_old, mb_scr[par, idx, :, qs] - shift)
        alpha = jnp.exp2(m_old - m_new)
        p = jnp.exp2(s_scr[par, idx, :, qs] - (m_new + shift))
        l_scr[idx, :, qs] = alpha * l_scr[idx, :, qs] + jnp.sum(p, axis=0, keepdims=True)
        vt_h = vt_ref[kb, hh * hd2:(hh + 1) * hd2, :]
        acc_scr[idx, :, qs] = alpha * acc_scr[idx, :, qs] + jnp.dot(
            vt_h, p.astype(BF16), preferred_element_type=F32)
        m_scr[idx, :, qs] = m_new

    def consume(kb, par):
        for idx, qs in tiles:
            consume_tile(kb, par, idx, qs)

    def stage(ki, par, kb):
        for idx, qs in tiles:
            score_tile(ki, par, idx, qs)
            consume_tile(kb, 1 - par, idx, qs)

    score_diag(0)

    def pair_body(u, carry):
        t0 = 2 * u
        stage(t0, 1, jnp.where(t0 == 0, qi, t0 - 1))
        stage(t0 + 1, 0, t0)
        return carry

    lax.fori_loop(0, qi // 2, pair_body, 0)

    @pl.when(qi % 2 == 1)
    def _():
        t0 = qi - 1
        stage(t0, 1, jnp.where(t0 == 0, qi, t0 - 1))
        consume(t0, 1)

    @pl.when(qi % 2 == 0)
    def _():
        consume(jnp.where(qi == 0, qi, qi - 1), 0)

    lam = _lambda_value(lq1, lk1, lq2, lk2, lam_init)
    for hh in range(2):
        o = acc_scr[hh] / l_scr[hh] - lam * (acc_scr[2 + hh] / l_scr[2 + hh])
        ms = jnp.mean(o * o, axis=0, keepdims=True)
        on = (o * lax.rsqrt(ms + NORM_EPS)).T
        sl = slice(hh * hd2, (hh + 1) * hd2)
        o_ref[:, sl] = (on * gn_ref[:, sl] * (1.0 - lam_init)).astype(BF16)


def _attention_prompt(p, qt, vt, lam_rows, da_gn_row, n_seq, seq_len, blk, lam_init):
    nb = seq_len // blk
    n_hp = DA_HEADS // 2
    hd2 = 2 * DA_HEAD_DIM
    kcol0 = COL_KD * COL_W // hd2
    ek, bq = _alibi_operands(blk)

    smem = pl.BlockSpec(memory_space=pltpu.SMEM)
    in_specs = [
        smem, smem,
        pl.BlockSpec((1, hd2, blk), lambda b, j, q: (b * nb + q, j, 0)),
        pl.BlockSpec((1, hd2, blk), lambda b, j, q: (b * nb + q, n_hp + j, 0)),
        pl.BlockSpec((seq_len, hd2), lambda b, j, q: (b, kcol0 + j)),
        pl.BlockSpec((seq_len, hd2), lambda b, j, q: (b, kcol0 + n_hp + j)),
        pl.BlockSpec((nb, 2 * hd2, blk), lambda b, j, q: (b, j, 0)),
        pl.BlockSpec((blk, hd2), lambda b, j, q: (0, 0)),
        pl.BlockSpec((hd2, blk), lambda b, j, q: (0, 0)),
    ] + [pl.BlockSpec((1, DA_HEAD_DIM), lambda b, j, q: (0, 0))] * 4 + [
        pl.BlockSpec((1, 2 * hd2), lambda b, j, q: (0, j)),
    ]
    return pl.pallas_call(
        functools.partial(_attn_kernel, blk=blk, lam_init=lam_init),
        grid=(n_seq, n_hp, nb), in_specs=in_specs,
        out_specs=pl.BlockSpec((blk, 2 * hd2), lambda b, j, q: (b * nb + q, j)),
        out_shape=jax.ShapeDtypeStruct((n_seq * seq_len, COL_W), BF16),
        scratch_shapes=[pltpu.VMEM((4, 2 * hd2, blk), BF16),
                        pltpu.VMEM((4, 1, blk), F32),
                        pltpu.VMEM((4, 1, blk), F32),
                        pltpu.VMEM((4, hd2, blk), F32),
                        pltpu.VMEM((2, 4, blk, blk), F32),
                        pltpu.VMEM((2, 4, 1, blk), F32),
                        pltpu.VMEM((2, blk, blk), F32)],
        compiler_params=_cparams(3), name="diff_attn_prompt",
    )(jnp.asarray(_SLOPES_LOG2), jnp.asarray(_SLOPES), qt, qt, p, p, vt, ek, bq,
      *lam_rows, da_gn_row)


def _sattn_kernel(slopes_ref, q_ref, kn_ref, vn_ref, kc_ref, vc_ref,
                  lq1, lk1, lq2, lk2, gn_ref, o_ref, *, L, P, lam_init):
    lam = _lambda_value(lq1, lk1, lq2, lk2, lam_init)
    hd2 = 2 * DA_HEAD_DIM
    lane = lax.broadcasted_iota(jnp.int32, (L, hd2), 1)
    qpos_c = P + lax.broadcasted_iota(jnp.int32, (L, P), 0)
    kpos_c = lax.broadcasted_iota(jnp.int32, (L, P), 1)
    qpos_n = P + lax.broadcasted_iota(jnp.int32, (L, L), 0)
    kpos_n = P + lax.broadcasted_iota(jnp.int32, (L, L), 1)
    dist_c = jnp.abs(qpos_c - kpos_c).astype(F32)
    dist_n = jnp.abs(qpos_n - kpos_n).astype(F32)
    all_allowed = P % CHUNK == 0 and L <= CHUNK
    ok_c = (kpos_c // CHUNK) <= (qpos_c // CHUNK)
    ok_n = (kpos_n // CHUNK) <= (qpos_n // CHUNK)
    nt = (((1,), (1,)), ((), ()))
    for pr in range(DA_HEADS // 2):
        sls = [slice((c * (DA_HEADS // 2) + pr) * hd2, (c * (DA_HEADS // 2) + pr + 1) * hd2)
               for c in range(2)]
        q128s = [q_ref[:, sl].astype(F32) * _Q_SCALE_LOG2 for sl in sls]
        kc_ts = [kc_ref[0, sl, :].astype(BF16) for sl in sls]
        for hh in range(2):
            h = 2 * pr + hh
            slope = slopes_ref[h]
            probs = []
            for c in range(2):
                keep = (lane < DA_HEAD_DIM) if hh == 0 else (lane >= DA_HEAD_DIM)
                qm = jnp.where(keep, q128s[c], 0.0).astype(BF16)
                s_c = jnp.dot(qm, kc_ts[c], preferred_element_type=F32) - slope * dist_c
                s_n = lax.dot_general(qm, kn_ref[:, sls[c]], nt,
                                      preferred_element_type=F32) - slope * dist_n
                if not all_allowed:
                    s_c = jnp.where(ok_c, s_c, _NEG)
                    s_n = jnp.where(ok_n, s_n, _NEG)
                m = jnp.maximum(jnp.max(s_c, axis=-1, keepdims=True),
                                jnp.max(s_n, axis=-1, keepdims=True))
                p_c = jnp.exp2(s_c - m)
                p_n = jnp.exp2(s_n - m)
                inv = 1.0 / (jnp.sum(p_c, axis=-1, keepdims=True)
                             + jnp.sum(p_n, axis=-1, keepdims=True))
                probs.append((p_c * inv, p_n * inv))
            a_c = (probs[0][0] - lam * probs[1][0]).astype(BF16)
            a_n = (probs[0][1] - lam * probs[1][1]).astype(BF16)
            vsl = slice(h * hd2, (h + 1) * hd2)
            vc = vc_ref[0, pl.ds(h, P, stride=DA_HEADS), :].astype(BF16)
            o = (jnp.dot(a_c, vc, preferred_element_type=F32)
                 + jnp.dot(a_n, vn_ref[:, vsl], preferred_element_type=F32))
            ms = jnp.mean(o * o, axis=-1, keepdims=True)
            on = o * lax.rsqrt(ms + NORM_EPS) * gn_ref[:, vsl] * (1.0 - lam_init)
            o_ref[:, vsl] = on.astype(BF16)


def _attention_sample(p, cache_kt, cache_v, lam_rows, da_gn_row, n_seq, L, lam_init):
    P = cache_kt.shape[2]
    smem = pl.BlockSpec(memory_space=pltpu.SMEM)
    in_specs = [
        smem,
        pl.BlockSpec((L, COL_W), lambda s: (s, COL_QD)),
        pl.BlockSpec((L, COL_W), lambda s: (s, COL_KD)),
        pl.BlockSpec((L, COL_W), lambda s: (s, COL_VD)),
        pl.BlockSpec((1, COL_W, P), lambda s: (s, 0, 0)),
        pl.BlockSpec((1, P * DA_HEADS, RET_DV), lambda s: (s, 0, 0)),
    ] + [pl.BlockSpec((1, DA_HEAD_DIM), lambda s: (0, 0))] * 4 + [
        pl.BlockSpec((1, COL_W), lambda s: (0, 0)),
    ]
    return pl.pallas_call(
        functools.partial(_sattn_kernel, L=L, P=P, lam_init=lam_init),
        grid=(n_seq,), in_specs=in_specs,
        out_specs=pl.BlockSpec((L, COL_W), lambda s: (s, 0)),
        out_shape=jax.ShapeDtypeStruct((n_seq * L, COL_W), BF16),
        compiler_params=_cparams(1), name="diff_attn_sample",
    )(jnp.asarray(_SLOPES_LOG2), p, p, p, cache_kt, cache_v, *lam_rows, da_gn_row)


def _merge_kernel(x_ref, yr_ref, yd_ref, gr0, gr1, gd0, gd1, wr_ref, wd_ref, wo_ref, gm_ref,
                  x1_ref, h2_ref):
    yr = jnp.dot(yr_ref[...], wr_ref[...], preferred_element_type=F32)
    yd = jnp.dot(yd_ref[...], wd_ref[...], preferred_element_type=F32)
    gr = jnp.concatenate([gr0[...], gr1[...]], axis=1).astype(F32)
    gd = jnp.concatenate([gd0[...], gd1[...]], axis=1).astype(F32)
    m = jax.nn.sigmoid(gr) * yr + jax.nn.sigmoid(gd) * yd
    x1 = x_ref[...] + jnp.dot(m.astype(BF16), wo_ref[...], preferred_element_type=F32)
    x1_ref[...] = x1
    ms = jnp.mean(x1 * x1, axis=-1, keepdims=True)
    h2_ref[...] = (x1 * lax.rsqrt(ms + NORM_EPS) * gm_ref[...]).astype(BF16)


def _merge(x2d, yr_in, yd_in, p, w_ret_o, w_da_o, w_out, g_mlp_row, tm):
    T = x2d.shape[0]

    def whole(shape):
        return pl.BlockSpec(shape, lambda i: (0, 0), pipeline_mode=pl.Buffered(1))

    def pcol(t):
        return pl.BlockSpec((tm, COL_W), lambda i: (i, t))

    in_specs = [
        pl.BlockSpec((tm, D_MODEL), lambda i: (i, 0)),
        pl.BlockSpec((tm, COL_W), lambda i: (i, 0)),
        pl.BlockSpec((tm, COL_W), lambda i: (i, 0)),
        pcol(COL_GR), pcol(COL_GR + 1), pcol(COL_GD), pcol(COL_GD + 1),
        whole((COL_W, D_MODEL)), whole((COL_W, D_MODEL)), whole((D_MODEL, D_MODEL)),
        pl.BlockSpec((1, D_MODEL), lambda i: (0, 0)),
    ]
    return pl.pallas_call(
        _merge_kernel, grid=(T // tm,), in_specs=in_specs,
        out_specs=[pl.BlockSpec((tm, D_MODEL), lambda i: (i, 0)),
                   pl.BlockSpec((tm, D_MODEL), lambda i: (i, 0))],
        out_shape=[jax.ShapeDtypeStruct((T, D_MODEL), F32),
                   jax.ShapeDtypeStruct((T, D_MODEL), BF16)],
        compiler_params=_cparams(1), name="merge",
    )(x2d, yr_in, yd_in, p, p, p, p, w_ret_o, w_da_o, w_out, g_mlp_row)


def _mlp_kernel(x1_ref, h2_ref, wu_ref, wd_ref, y_ref, r_scr, *, n_steps, nf):
    t = pl.program_id(0)
    fb = (jnp.maximum(t, 1) - 1) % nf

    def up(par):
        u = jnp.dot(h2_ref[...], wu_ref[...], preferred_element_type=F32)
        r_scr[par] = jnp.square(jnp.maximum(u, 0.0)).astype(BF16)

    def down(par, first_chunk):
        contrib = jnp.dot(r_scr[par], wd_ref[...], preferred_element_type=F32)
        if first_chunk:
            y_ref[...] = x1_ref[...] + contrib
        else:
            y_ref[...] += contrib

    mid = (t > 0) & (t < n_steps - 1)

    @pl.when(t == 0)
    def _():
        up(0)

    @pl.when(mid & (t % 2 == 0))
    def _():
        down(1, False)
        up(0)

    @pl.when(mid & (t % 2 == 1) & (fb == 0))
    def _():
        down(0, True)
        up(1)

    @pl.when(mid & (t % 2 == 1) & (fb != 0))
    def _():
        down(0, False)
        up(1)

    @pl.when(t == n_steps - 1)
    def _():
        down((n_steps - 2) % 2, False)


def _mlp(x1, h2, w_up, w_down, tm, tf):
    T = x1.shape[0]
    nf = D_FF // tf
    assert nf % 2 == 0 and nf >= 2
    n_up = (T // tm) * nf
    n_steps = n_up + 1

    def up_idx(t):
        return jnp.minimum(t, n_up - 1)

    def down_idx(t):
        return jnp.maximum(t, 1) - 1

    return pl.pallas_call(
        functools.partial(_mlp_kernel, n_steps=n_steps, nf=nf), grid=(n_steps,),
        in_specs=[pl.BlockSpec((tm, D_MODEL), lambda t: (down_idx(t) // nf, 0)),
                  pl.BlockSpec((tm, D_MODEL), lambda t: (up_idx(t) // nf, 0)),
                  pl.BlockSpec((D_MODEL, tf), lambda t: (0, up_idx(t) % nf)),
                  pl.BlockSpec((tf, D_MODEL), lambda t: (down_idx(t) % nf, 0))],
        out_specs=pl.BlockSpec((tm, D_MODEL), lambda t: (down_idx(t) // nf, 0)),
        out_shape=jax.ShapeDtypeStruct((T, D_MODEL), F32),
        scratch_shapes=[pltpu.VMEM((2, tm, tf), BF16)],
        compiler_params=_cparams(1), name="mlp",
    )(x1, h2, w_up, w_down)


def _block_diag_mean():
    r = np.arange(V7X_MXU_DIM) // DA_HEAD_DIM
    return jnp.asarray((r[:, None] == r[None, :]).astype(np.float32) / DA_HEAD_DIM, BF16)


def kernel(x_prompt, x_sample, cache_da_k, cache_da_v, state_ret, g_mix, w_in, ret_gn, qn_g, kn_g,
           lambda_q1, lambda_k1, lambda_q2, lambda_k2, da_gn, w_ret_o, w_da_o, w_out, g_mlp,
           w_up, w_down):
    B, S, _ = x_prompt.shape
    DB, L, _ = x_sample.shape
    P = cache_da_k.shape[2]
    depth = w_in.shape[0]
    assert S % TOKEN_TILE == 0 and S % RET_CHUNK_PROMPT == 0 and L % 16 == 0

    tm_p = TOKEN_TILE
    tm_s = _pick_tile(DB * L, TOKEN_TILE)
    tm_ip = _pick_tile(S, INPROJ_TILE)
    tm_is = _pick_tile(DB * L, INPROJ_TILE)
    assert tm_ip % TOKEN_TILE == 0 and (tm_is % L == 0 or L % tm_is == 0)
    rope_p = _rope_operands(np.arange(S // tm_ip) * tm_ip, S // tm_ip, np.arange(tm_ip))
    if tm_is >= L:
        rope_s = _rope_operands([0], 1, P + np.arange(tm_is) % L)
    else:
        rope_s = _rope_operands(P + np.arange(L // tm_is) * tm_is, L // tm_is, np.arange(tm_is))
    bd = _block_diag_mean()
    n_heads_qk = COL_W // DA_HEAD_DIM

    xp = x_prompt.reshape(B * S, D_MODEL)
    xs = x_sample.reshape(DB * L, D_MODEL)
    kp_l, vp_l, rp_l, ks_l, vs_l, rs_l = [], [], [], [], [], []
    for l in range(depth):
        lam_init = 0.8 - 0.6 * math.exp(-0.3 * l)
        w_in_bf = w_in[l].astype(BF16)
        w_ro, w_do, w_o = w_ret_o[l].astype(BF16), w_da_o[l].astype(BF16), w_out[l].astype(BF16)
        w_u, w_d = w_up[l].astype(BF16), w_down[l].astype(BF16)
        g_mix_row = g_mix[l].reshape(1, D_MODEL)
        g_mlp_row = g_mlp[l].reshape(1, D_MODEL)
        qg16 = jnp.tile(qn_g[l], n_heads_qk).reshape(1, COL_W)
        kg16 = jnp.tile(kn_g[l], n_heads_qk).reshape(1, COL_W)
        ret_gn_row = ret_gn[l].reshape(1, COL_W)
        da_gn_row = da_gn[l].reshape(1, COL_W)
        lam_rows = [a[l].astype(F32).reshape(1, DA_HEAD_DIM)
                    for a in (lambda_q1, lambda_k1, lambda_q2, lambda_k2)]

        h, p = _inproj0(xp, g_mix_row, w_in_bf, rope_p, tm_ip)
        p, nkt, nv, qt, vt = _inproj(h, p, w_in_bf, rope_p, qg16, kg16, bd, tm_ip, S)
        yr_in, s_fin = _retention(p, ret_gn_row, B, S, RET_CHUNK_PROMPT, None)
        yd_in = _attention_prompt(p, qt, vt, lam_rows, da_gn_row, B, S, TOKEN_TILE, lam_init)
        x1, h2 = _merge(xp, yr_in, yd_in, p, w_ro, w_do, w_o, g_mlp_row, tm_p)
        xp = _mlp(x1, h2, w_u, w_d, tm_p, MLP_FF_TILE)
        nk = jnp.transpose(nkt.reshape(B, 2 * DA_HEADS, DA_HEAD_DIM, S), (0, 3, 1, 2))
        kp_l.append(nk)
        vp_l.append(nv.reshape(B, S, DA_HEADS, 2 * DA_HEAD_DIM))
        rp_l.append(s_fin)

        h, p = _inproj0(xs, g_mix_row, w_in_bf, rope_s, tm_is)
        p, nk, nv = _inproj(h, p, w_in_bf, rope_s, qg16, kg16, bd, tm_is, None)
        yr_in, s_new = _retention(p, ret_gn_row, DB, L, L, state_ret[l].astype(F32))
        cache_kt = jnp.transpose(cache_da_k[l].reshape(DB, P, COL_W), (0, 2, 1))
        yd_in = _attention_sample(p, cache_kt, cache_da_v[l].reshape(DB, P * DA_HEADS, RET_DV),
                                  lam_rows, da_gn_row, DB, L, lam_init)
        x1, h2 = _merge(xs, yr_in, yd_in, p, w_ro, w_do, w_o, g_mlp_row, tm_s)
        xs = _mlp(x1, h2, w_u, w_d, tm_s, MLP_FF_TILE)
        ks_l.append(nk.reshape(DB, L, 2 * DA_HEADS, DA_HEAD_DIM))
        vs_l.append(nv.reshape(DB, L, DA_HEADS, 2 * DA_HEAD_DIM))
        rs_l.append(s_new)

    return (xp.reshape(B, S, D_MODEL), xs.reshape(DB, L, D_MODEL),
            jnp.stack(kp_l, 0), jnp.stack(vp_l, 0), jnp.stack(rp_l, 0),
            jnp.stack(ks_l, 0), jnp.stack(vs_l, 0), jnp.stack(rs_l, 0))
```
